```python
import jax, jax.numpy as jnp
from jax import lax
import numpy as np

D_MODEL = 1024
BATCH = 16
SEQ = 2048
DEPTH = 2

GRID_W = 64
CTX_LEN = 256
HEAD_DIM = 64
N_Q_HEADS = 8
N_KV_HEADS = 2
Q_GROUP = N_Q_HEADS // N_KV_HEADS
ATTN_W = N_Q_HEADS * HEAD_DIM
KV_W = N_KV_HEADS * HEAD_DIM
POOL_WINDOWS = (2, 4, 8, 16)
N_POOL_GROUPS = len(POOL_WINDOWS)
POOL_W = D_MODEL // 2
POOL_GROUP_W = POOL_W // N_POOL_GROUPS
CONV_W = D_MODEL // 2
CONV_K = 31
N_BRANCHES = 3
Q_END = ATTN_W
K_END = Q_END + KV_W
V_END = K_END + KV_W
P_END = V_END + POOL_W
C_END = P_END + 2 * CONV_W
IN_W = C_END + N_BRANCHES * D_MODEL
D_FF = -(-8 * D_MODEL // (3 * 256)) * 256
Q_BLOCK = 128
ROPE_THETA = 10000.0
EPS = 1e-6

kernel_name = 'hybrid_gqa_pool_conformer_prefix_dit'


def rmsnorm(x, g):
    xf = x.astype(jnp.float32)
    y = xf * lax.rsqrt(jnp.mean(xf * xf, axis=-1, keepdims=True) + EPS)
    return (y * g.astype(jnp.float32)).astype(x.dtype)


def layernorm(x, g, b):
    xf = x.astype(jnp.float32)
    mu = jnp.mean(xf, axis=-1, keepdims=True)
    var = jnp.mean(jnp.square(xf - mu), axis=-1, keepdims=True)
    y = (xf - mu) * lax.rsqrt(var + EPS)
    return (y * g.astype(jnp.float32) + b.astype(jnp.float32)).astype(x.dtype)


def axial_rope_tables(n_tokens):
    rows = n_tokens // GRID_W
    row = jnp.repeat(jnp.arange(rows), GRID_W).astype(jnp.float32)
    col = jnp.tile(jnp.arange(GRID_W), rows).astype(jnp.float32)
    n_freq = HEAD_DIM // 4
    inv = 1.0 / (ROPE_THETA ** (jnp.arange(n_freq, dtype=jnp.float32) / n_freq))
    ang_r = row[:, None] * inv[None, :]
    ang_c = col[:, None] * inv[None, :]
    return (jnp.cos(ang_r), jnp.sin(ang_r), jnp.cos(ang_c), jnp.sin(ang_c))


def _rotate(u, cos, sin):
    half = u.shape[-1] // 2
    u1, u2 = u[..., :half], u[..., half:]
    cos = cos[None, :, None, :]
    sin = sin[None, :, None, :]
    return jnp.concatenate([u1 * cos - u2 * sin, u2 * cos + u1 * sin], axis=-1)


def apply_axial_rope(x, tabs):
    cos_r, sin_r, cos_c, sin_c = tabs
    xf = x.astype(jnp.float32)
    half = HEAD_DIM // 2
    out = jnp.concatenate([_rotate(xf[..., :half], cos_r, sin_r),
                           _rotate(xf[..., half:], cos_c, sin_c)], axis=-1)
    return out.astype(x.dtype)


def split_heads(u, n_heads):
    return u.reshape(u.shape[:-1] + (n_heads, HEAD_DIM))


def kv_heads(u_kv, k_norm, rope):
    k = rmsnorm(split_heads(u_kv[..., :KV_W], N_KV_HEADS), k_norm)
    v = split_heads(u_kv[..., KV_W:], N_KV_HEADS)
    if rope is not None:
        k = apply_axial_rope(k, rope)
    return k, v


def attention(q, k, v):
    B, L = q.shape[0], q.shape[1]
    nblk = L // Q_BLOCK
    qb = (q * HEAD_DIM ** -0.5).reshape(B, nblk, Q_BLOCK, N_KV_HEADS, Q_GROUP, HEAD_DIM)
    qb = qb.transpose(1, 0, 2, 3, 4, 5)

    def one_block(qi):
        s = jnp.einsum('bqhgd,bshd->bhgqs', qi, k).astype(jnp.float32)
        p = jax.nn.softmax(s, axis=-1).astype(v.dtype)
        return jnp.einsum('bhgqs,bshd->bqhgd', p, v)

    o = lax.map(one_block, qb)
    return o.transpose(1, 0, 2, 3, 4, 5).reshape(B, L, ATTN_W)


def multiscale_pool(u, pool_w, pool_scale):
    B, L, _ = u.shape
    uf = u.astype(jnp.float32)
    cs = jnp.concatenate([jnp.zeros((B, 1, POOL_W), jnp.float32), jnp.cumsum(uf, axis=1)], axis=1)
    t = jnp.arange(L)
    means = []
    for gi, w in enumerate(POOL_WINDOWS):
        lo = jnp.clip(t - w // 2, 0, L)
        hi = jnp.clip(t + w // 2, 0, L)
        seg = cs[:, :, gi * POOL_GROUP_W:(gi + 1) * POOL_GROUP_W]
        cnt = (hi - lo).astype(jnp.float32)[None, :, None]
        means.append((seg[:, hi] - seg[:, lo]) / cnt)
    d = (jnp.concatenate(means, axis=-1) - uf).astype(u.dtype)
    d = d.reshape(B, L, N_POOL_GROUPS, POOL_GROUP_W)
    y = jnp.einsum('blgc,gcd->blgd', d, pool_w).reshape(B, L, POOL_W)
    return y * pool_scale


def conformer_conv(u, conv_dw, conv_b, ln_g, ln_b, w_o):
    a, gt = u[..., :CONV_W], u[..., CONV_W:]
    g = a * jax.nn.sigmoid(gt)
    y = lax.conv_general_dilated(g, conv_dw[:, None, :], window_strides=(1,),
                                 padding=[(CONV_K // 2, CONV_K // 2)],
                                 dimension_numbers=('NWC', 'WIO', 'NWC'),
                                 feature_group_count=CONV_W) + conv_b
    y = jax.nn.silu(layernorm(y, ln_g, ln_b))
    return y @ w_o


def mixer_out(q, k, v, u_pool, u_conv, u_gate, lp):
    ya = attention(q, k, v) @ lp['w_attn_o']
    yb = multiscale_pool(u_pool, lp['pool_w'], lp['pool_scale']) @ lp['w_pool_o']
    yc = conformer_conv(u_conv, lp['conv_dw'], lp['conv_b'], lp['conv_ln_g'], lp['conv_ln_b'], lp['w_conv_o'])
    g = jax.nn.sigmoid(u_gate.reshape(u_gate.shape[:-1] + (N_BRANCHES, D_MODEL)))
    merged = g[..., 0, :] * ya + g[..., 1, :] * yb + g[..., 2, :] * yc
    return merged @ lp['w_out']


def swiglu(h, w_ffn_in, w_ffn_out):
    u = h @ w_ffn_in
    return (jax.nn.silu(u[..., :D_FF]) * u[..., D_FF:]) @ w_ffn_out


def setup_inputs(seed: int = 0) -> dict:
    key = jax.random.key(seed)
    ks = jax.random.split(key, 32)
    f32 = jnp.float32
    D = D_MODEL

    def nrm(k, shape, scale):
        return jax.random.normal(k, shape, f32) * scale

    return {
        'x': nrm(ks[0], (BATCH, SEQ, D), 1.0),
        'c': nrm(ks[1], (BATCH, D), 1.0),
        'ctx': nrm(ks[2], (BATCH, CTX_LEN, D), 1.0),
        'c_ctx': nrm(ks[3], (D,), 1.0),
        'w_ada': nrm(ks[4], (DEPTH, D, 6 * D), 0.5 * D ** -0.5),
        'b_ada': nrm(ks[5], (DEPTH, 6 * D), 0.01),
        'g_pre1': 1.0 + nrm(ks[6], (DEPTH, D), 0.05),
        'g_post1': 1.0 + nrm(ks[7], (DEPTH, D), 0.05),
        'g_pre2': 1.0 + nrm(ks[8], (DEPTH, D), 0.05),
        'g_post2': 1.0 + nrm(ks[9], (DEPTH, D), 0.05),
        'w_in': nrm(ks[10], (DEPTH, D, IN_W), D ** -0.5),
        'q_norm': 1.0 + nrm(ks[11], (DEPTH, HEAD_DIM), 0.05),
        'k_norm': 1.0 + nrm(ks[12], (DEPTH, HEAD_DIM), 0.05),
        'w_attn_o': nrm(ks[13], (DEPTH, ATTN_W, D), ATTN_W ** -0.5),
        'pool_w': nrm(ks[14], (DEPTH, N_POOL_GROUPS, POOL_GROUP_W, POOL_GROUP_W), POOL_GROUP_W ** -0.5),
        'pool_scale': 1.0 + nrm(ks[15], (DEPTH, POOL_W), 0.05),
        'w_pool_o': nrm(ks[16], (DEPTH, POOL_W, D), POOL_W ** -0.5),
        'conv_dw': nrm(ks[17], (DEPTH, CONV_K, CONV_W), CONV_K ** -0.5),
        'conv_b': nrm(ks[18], (DEPTH, CONV_W), 0.01),
        'conv_ln_g': 1.0 + nrm(ks[19], (DEPTH, CONV_W), 0.05),
        'conv_ln_b': nrm(ks[20], (DEPTH, CONV_W), 0.01),
        'w_conv_o': nrm(ks[21], (DEPTH, CONV_W, D), CONV_W ** -0.5),
        'w_out': nrm(ks[22], (DEPTH, D, D), D ** -0.5),
        'w_ffn_in': nrm(ks[23], (DEPTH, D, 2 * D_FF), D ** -0.5),
        'w_ffn_out': nrm(ks[24], (DEPTH, D_FF, D), D_FF ** -0.5),
    }


def reference(x, c, ctx, c_ctx, w_ada, b_ada, g_pre1, g_post1, g_pre2, g_post2, w_in, q_norm, k_norm,
              w_attn_o, pool_w, pool_scale, w_pool_o, conv_dw, conv_b, conv_ln_g, conv_ln_b, w_conv_o,
              w_out, w_ffn_in, w_ffn_out):
    n_lat = x.shape[1]
    rope = axial_rope_tables(n_lat)
    e = ctx
    for i in range(DEPTH):
        last = i == DEPTH - 1
        lp = {'w_attn_o': w_attn_o[i], 'pool_w': pool_w[i], 'pool_scale': pool_scale[i],
              'w_pool_o': w_pool_o[i], 'conv_dw': conv_dw[i], 'conv_b': conv_b[i],
              'conv_ln_g': conv_ln_g[i], 'conv_ln_b': conv_ln_b[i], 'w_conv_o': w_conv_o[i],
              'w_out': w_out[i]}
        mod_x = (jax.nn.silu(c) @ w_ada[i] + b_ada[i])[:, None, :]
        mod_c = (jax.nn.silu(c_ctx) @ w_ada[i] + b_ada[i])[None, None, :]
        sh1, sc1, gt1, sh2, sc2, gt2 = jnp.split(mod_x, 6, axis=-1)
        csh1, csc1, cgt1, csh2, csc2, cgt2 = jnp.split(mod_c, 6, axis=-1)

        hx = rmsnorm(x, g_pre1[i]) * (1.0 + sc1) + sh1
        hc = rmsnorm(e, g_pre1[i]) * (1.0 + csc1) + csh1
        ux = hx @ w_in[i]
        qx = apply_axial_rope(rmsnorm(split_heads(ux[..., :Q_END], N_Q_HEADS), q_norm[i]), rope)
        kx, vx = kv_heads(ux[..., Q_END:V_END], k_norm[i], rope)
        if last:
            uc_kv = hc @ w_in[i][:, Q_END:V_END]
        else:
            uc = hc @ w_in[i]
            uc_kv = uc[..., Q_END:V_END]
        kc, vc = kv_heads(uc_kv, k_norm[i], None)
        k_all = jnp.concatenate([kc, kx], axis=1)
        v_all = jnp.concatenate([vc, vx], axis=1)
        yx = mixer_out(qx, k_all, v_all, ux[..., V_END:P_END], ux[..., P_END:C_END], ux[..., C_END:], lp)
        x_new = x + gt1 * rmsnorm(yx, g_post1[i])
        if not last:
            qc = rmsnorm(split_heads(uc[..., :Q_END], N_Q_HEADS), q_norm[i])
            yc = mixer_out(qc, kc, vc, uc[..., V_END:P_END], uc[..., P_END:C_END], uc[..., C_END:], lp)
            e = e + cgt1 * rmsnorm(yc, g_post1[i])
        x = x_new

        hx2 = rmsnorm(x, g_pre2[i]) * (1.0 + sc2) + sh2
        x = x + gt2 * rmsnorm(swiglu(hx2, w_ffn_in[i], w_ffn_out[i]), g_post2[i])
        if not last:
            hc2 = rmsnorm(e, g_pre2[i]) * (1.0 + csc2) + csh2
            e = e + cgt2 * rmsnorm(swiglu(hc2, w_ffn_in[i], w_ffn_out[i]), g_post2[i])
    return x
```

```python
import functools

import jax
import jax.numpy as jnp
from jax import lax
from jax.experimental import pallas as pl
from jax.experimental.pallas import tpu as pltpu

D_MODEL = 1024
HEAD_DIM = 64
N_Q_HEADS = 8
N_KV_HEADS = 2
Q_GROUP = N_Q_HEADS // N_KV_HEADS
ATTN_W = N_Q_HEADS * HEAD_DIM
KV_W = N_KV_HEADS * HEAD_DIM
GROUP_W = Q_GROUP * HEAD_DIM
POOL_WINDOWS = (2, 4, 8, 16)
POOL_W = D_MODEL // 2
POOL_GROUP_W = POOL_W // len(POOL_WINDOWS)
CONV_W = D_MODEL // 2
CONV_K = 31
GRID_W = 64
ROPE_THETA = 10000.0
EPS = 1e-6
Q_END = ATTN_W
K_END = Q_END + KV_W
V_END = K_END + KV_W
P_END = V_END + POOL_W
C_END = P_END + 2 * CONV_W
D_FF = -(-8 * D_MODEL // (3 * 256)) * 256
FF_CHUNKS = ((0, 1024), (1024, 2048), (2048, D_FF))

V7X_LANES = 128
V7X_SUBLANES = 8
POOL_HALO = 8
CONV_HALO = 16
CONV_ROWS = 32
MOD_ROWS = 24
VMEM_LIMIT = 56 * 1024 * 1024

F32 = jnp.float32
BF16 = jnp.bfloat16


def _const_spec(shape):
    return pl.BlockSpec(shape, lambda *_: (0,) * len(shape), pipeline_mode=pl.Buffered(1))


def _params(n_axes):
    return pltpu.CompilerParams(dimension_semantics=("arbitrary",) * n_axes,
                                vmem_limit_bytes=VMEM_LIMIT)


def _dot(a, b):
    return jnp.dot(a, b, preferred_element_type=F32)


def _modulated_rmsnorm(x, g, shift, scale):
    y = x * lax.rsqrt(jnp.mean(x * x, axis=-1, keepdims=True) + EPS)
    return (y * g) * (1.0 + scale) + shift


def _ada_kernel(c_ref, w_ref, b_ref, o_ref):
    c = c_ref[...]
    s = c * jax.nn.sigmoid(c)
    o_ref[...] = jnp.dot(s, w_ref[...], preferred_element_type=F32,
                         precision=lax.Precision.HIGHEST) + b_ref[...]


def _ada(cc, w_ada, b_ada):
    depth, d, n = w_ada.shape
    bn = 1536
    return pl.pallas_call(
        _ada_kernel,
        out_shape=jax.ShapeDtypeStruct((depth, MOD_ROWS, n), F32),
        grid=(depth, n // bn),
        in_specs=[pl.BlockSpec((MOD_ROWS, d), lambda l, j: (0, 0)),
                  pl.BlockSpec((None, d, bn), lambda l, j: (l, 0, j)),
                  pl.BlockSpec((None, 1, bn), lambda l, j: (l, 0, j))],
        out_specs=pl.BlockSpec((None, MOD_ROWS, bn), lambda l, j: (l, 0, j)),
        compiler_params=_params(2),
        name="adaln",
    )(cc, w_ada, b_ada.reshape(depth, 1, n))


def _head_rms(u, headmat, gain):
    ms = _dot((u * u).astype(BF16), headmat)
    return u * lax.rsqrt(ms + EPS) * gain


def _rope(u, cos, sin_signed):
    lane = lax.broadcasted_iota(jnp.int32, u.shape, 1)
    first_half = (lane % 32) < 16
    partner = jnp.where(first_half,
                        pltpu.roll(u, V7X_LANES - 16, 1),
                        pltpu.roll(u, 16, 1))
    return u * cos + partner * sin_signed


def _replicate_heads(a):
    lane = lax.broadcasted_iota(jnp.int32, a.shape, 1)
    low = lane < HEAD_DIM
    r = pltpu.roll(a, HEAD_DIM, 1)
    return jnp.where(low, a, r), jnp.where(low, r, a)


def _in_proj_kernel(*refs, rope, kv_only):
    it = iter(refs)
    x_ref, mod_ref, gpre_ref, w_ref, hm_ref, qg_ref, kg_ref = (next(it) for _ in range(7))
    cos_ref = sin_ref = None
    if rope:
        cos_ref, sin_ref = next(it), next(it)
    if kv_only:
        k_out, v_out = next(it), next(it)
    else:
        q_out, k_out, v_out, up_out, gc_out = (next(it) for _ in range(5))

    d = D_MODEL
    h = _modulated_rmsnorm(x_ref[...], gpre_ref[...], mod_ref[:, 0:d], mod_ref[:, d:2 * d])
    hb = h.astype(BF16)

    if kv_only:
        ukv = _dot(hb, w_ref[...])
    else:
        ukv = _dot(hb, w_ref[:, Q_END:V_END])
    k = _head_rms(ukv[:, :KV_W], hm_ref[0:KV_W, 0:KV_W], kg_ref[...])
    if rope:
        k = _rope(k, cos_ref[...], sin_ref[...])
    for out, val in ((k_out, k), (v_out, ukv[:, KV_W:])):
        h0, h1 = _replicate_heads(val)
        h0 = h0.astype(BF16)
        h1 = h1.astype(BF16)
        out[:, 0:128] = h0
        out[:, 128:256] = h0
        out[:, 256:384] = h1
        out[:, 384:512] = h1
    if kv_only:
        return

    q = _head_rms(_dot(hb, w_ref[:, 0:Q_END]), hm_ref[...], qg_ref[...])
    for c in range(ATTN_W // V7X_LANES):
        sl = slice(c * V7X_LANES, (c + 1) * V7X_LANES)
        qc = q[:, sl]
        if rope:
            qc = _rope(qc, cos_ref[...], sin_ref[...])
        q_out[:, sl] = (qc * HEAD_DIM ** -0.5).astype(BF16)

    up_out[...] = _dot(hb, w_ref[:, V_END:P_END])
    uc = _dot(hb, w_ref[:, P_END:C_END])
    gc_out[...] = uc[:, :CONV_W] * jax.nn.sigmoid(uc[:, CONV_W:])


def _in_proj(x, mod, mod_row, g_pre, w, headmat, q_gain, k_gain, rope_tabs, *, tile, kv_only=False):
    bn, ln, d = x.shape
    rope = rope_tabs is not None
    if mod_row is None:
        mod_map = lambda b, j: (b, 0, 0)
    else:
        mod_map = lambda b, j: (mod_row, 0, 0)
    in_specs = [pl.BlockSpec((None, tile, d), lambda b, j: (b, j, 0)),
                pl.BlockSpec((None, 1, mod.shape[-1]), mod_map),
                _const_spec(g_pre.shape), _const_spec(w.shape), _const_spec(headmat.shape),
                _const_spec(q_gain.shape), _const_spec(k_gain.shape)]
    args = [x, mod, g_pre, w, headmat, q_gain, k_gain]
    if rope:
        in_specs += [pl.BlockSpec((tile, V7X_LANES), lambda b, j: (j, 0))] * 2
        args += list(rope_tabs)

    def tok_spec(width):
        return pl.BlockSpec((None, tile, width), lambda b, j: (b, j, 0))

    rep = jax.ShapeDtypeStruct((bn, ln, 2 * GROUP_W), BF16)
    if kv_only:
        out_shape = (rep, rep)
        out_specs = (tok_spec(2 * GROUP_W),) * 2
    else:
        out_shape = (jax.ShapeDtypeStruct((bn, ln, ATTN_W), BF16), rep, rep,
                     jax.ShapeDtypeStruct((bn, ln, POOL_W), F32),
                     jax.ShapeDtypeStruct((bn, ln, CONV_W), F32))
        out_specs = (tok_spec(ATTN_W), tok_spec(2 * GROUP_W), tok_spec(2 * GROUP_W),
                     tok_spec(POOL_W), tok_spec(CONV_W))
    return pl.pallas_call(
        functools.partial(_in_proj_kernel, rope=rope, kv_only=kv_only),
        out_shape=out_shape, grid=(bn, ln // tile), in_specs=in_specs, out_specs=out_specs,
        compiler_params=_params(2),
        name="in_proj_kv" if kv_only else "in_proj",
    )(*args)


def _attn_kernel(*refs, n_src):
    q_ref = refs[0]
    k_refs = refs[1:1 + n_src]
    v_refs = refs[1 + n_src:1 + 2 * n_src]
    o_ref = refs[1 + 2 * n_src]
    acc_ref = refs[2 + 2 * n_src]

    q = q_ref[...]
    head_of_lane = lax.broadcasted_iota(jnp.int32, q.shape, 1) // HEAD_DIM
    acc_ref[...] = jnp.zeros_like(acc_ref)

    def one_head(g, carry):
        mine = head_of_lane == g
        qg = jnp.where(mine, q, jnp.zeros_like(q))
        s = [lax.dot_general(qg, k[...], (((1,), (1,)), ((), ())), preferred_element_type=F32)
             for k in k_refs]
        m = functools.reduce(jnp.maximum, [jnp.max(si, axis=-1, keepdims=True) for si in s])
        p = [jnp.exp(si - m) for si in s]
        denom = functools.reduce(jnp.add, [jnp.sum(pi, axis=-1, keepdims=True) for pi in p])
        og = functools.reduce(jnp.add, [_dot(pi.astype(BF16), v[...]) for pi, v in zip(p, v_refs)])
        acc_ref[...] = jnp.where(mine, og / denom, acc_ref[...])
        return carry

    lax.fori_loop(0, Q_GROUP, one_head, 0)
    o_ref[...] = acc_ref[...].astype(o_ref.dtype)


def _attention(q, k_srcs, v_srcs, *, tile):
    bn, ln, _ = q.shape
    n_src = len(k_srcs)
    q_spec = pl.BlockSpec((None, tile, GROUP_W), lambda b, h, i: (b, i, h))
    kv_specs = [pl.BlockSpec((None, k.shape[1], GROUP_W), lambda b, h, i: (b, 0, h))
                for k in tuple(k_srcs) + tuple(v_srcs)]
    return pl.pallas_call(
        functools.partial(_attn_kernel, n_src=n_src),
        out_shape=jax.ShapeDtypeStruct((bn, ln, ATTN_W), BF16),
        grid=(bn, N_KV_HEADS, ln // tile),
        in_specs=[q_spec] + kv_specs, out_specs=q_spec,
        scratch_shapes=[pltpu.VMEM((tile, GROUP_W), F32)],
        compiler_params=_params(3),
        name="attention",
    )(q, *k_srcs, *v_srcs)


def _mixer_kernel(x_ref, o_ref, up_ref, up_prev_ref, up_next_ref, gc_ref, gc_prev_ref, gc_next_ref,
                  mod_ref, gpre_ref, gpost_ref, wgate_ref, wao_ref, poolw_ref, pscale_ref, wpo_ref,
                  cdw_ref, cb_ref, lng_ref, lnb_ref, wco_ref, wout_ref,
                  out_ref, ppad_ref, gpad_ref, gsh_ref, ybuf_ref, *, seq_len):
    tile = x_ref.shape[0]
    d = D_MODEL
    j = pl.program_id(1)
    first = j == 0
    last = j == pl.num_programs(1) - 1

    x = x_ref[...]
    hb = _modulated_rmsnorm(x, gpre_ref[...], mod_ref[:, 0:d], mod_ref[:, d:2 * d]).astype(BF16)

    merged = jax.nn.sigmoid(_dot(hb, wgate_ref[:, 0:d])) * _dot(o_ref[...], wao_ref[...])

    ppad_ref[0:POOL_HALO, :] = jnp.where(first, 0.0, up_prev_ref[...])
    ppad_ref[POOL_HALO:POOL_HALO + tile, :] = up_ref[...]
    ppad_ref[POOL_HALO + tile:, :] = jnp.where(last, 0.0, up_next_ref[...])
    t_abs = j * tile + lax.broadcasted_iota(jnp.int32, (tile, POOL_GROUP_W), 0)
    pooled = []
    for gi, w in enumerate(POOL_WINDOWS):
        cols = slice(gi * POOL_GROUP_W, (gi + 1) * POOL_GROUP_W)
        tot = ppad_ref[POOL_HALO - w // 2:POOL_HALO - w // 2 + tile, cols]
        for off in range(-w // 2 + 1, w // 2):
            tot = tot + ppad_ref[POOL_HALO + off:POOL_HALO + off + tile, cols]
        cnt = jnp.minimum(t_abs + w // 2, seq_len) - jnp.maximum(t_abs - w // 2, 0)
        dg = tot / cnt.astype(F32) - ppad_ref[POOL_HALO:POOL_HALO + tile, cols]
        pooled.append(_dot(dg.astype(BF16), poolw_ref[gi]))
    yb = jnp.concatenate(pooled, axis=-1) * pscale_ref[...]
    merged = merged + jax.nn.sigmoid(_dot(hb, wgate_ref[:, d:2 * d])) * _dot(yb.astype(BF16), wpo_ref[...])

    gpad_ref[0:CONV_HALO, :] = jnp.where(first, 0.0, gc_prev_ref[...])
    gpad_ref[CONV_HALO:CONV_HALO + tile, :] = gc_ref[...]
    gpad_ref[CONV_HALO + tile:, :] = jnp.where(last, 0.0, gc_next_ref[...])
    tap0 = CONV_HALO - CONV_K // 2
    shifted_rows = gsh_ref.shape[1]
    for r in range(V7X_SUBLANES):
        gsh_ref[r] = gpad_ref[r:r + shifted_rows, :]

    def conv_chunk(c, carry):
        r0 = pl.multiple_of(c * CONV_ROWS, CONV_ROWS)
        acc = jnp.broadcast_to(cb_ref[...], (CONV_ROWS, CONV_W))
        for k in range(CONV_K):
            phase = (tap0 + k) % V7X_SUBLANES
            base = (tap0 + k) - phase
            acc = acc + cdw_ref[k:k + 1, :] * gsh_ref[phase, pl.ds(r0 + base, CONV_ROWS), :]
        mu = jnp.mean(acc, axis=-1, keepdims=True)
        cen = acc - mu
        var = jnp.mean(cen * cen, axis=-1, keepdims=True)
        y = cen * lax.rsqrt(var + EPS) * lng_ref[...] + lnb_ref[...]
        ybuf_ref[pl.ds(r0, CONV_ROWS), :] = (y * jax.nn.sigmoid(y)).astype(BF16)
        return carry

    lax.fori_loop(0, tile // CONV_ROWS, conv_chunk, 0)
    merged = merged + jax.nn.sigmoid(_dot(hb, wgate_ref[:, 2 * d:3 * d])) * _dot(ybuf_ref[...], wco_ref[...])

    y = _dot(merged.astype(BF16), wout_ref[...])
    yn = y * lax.rsqrt(jnp.mean(y * y, axis=-1, keepdims=True) + EPS) * gpost_ref[...]
    out_ref[...] = x + mod_ref[:, 2 * d:3 * d] * yn


def _mixer(x, attn, up, gc, mod, mod_row, lw, *, tile):
    bn, ln, d = x.shape
    n_tiles = ln // tile
    if mod_row is None:
        mod_map = lambda b, j: (b, 0, 0)
    else:
        mod_map = lambda b, j: (mod_row, 0, 0)

    def tok_spec(width):
        return pl.BlockSpec((None, tile, width), lambda b, j: (b, j, 0))

    def halo_specs(width, halo):
        per_tile = tile // halo
        n_blocks = ln // halo
        prev = pl.BlockSpec((None, halo, width), lambda b, j: (b, jnp.maximum(j * per_tile - 1, 0), 0))
        nxt = pl.BlockSpec((None, halo, width),
                           lambda b, j: (b, jnp.minimum((j + 1) * per_tile, n_blocks - 1), 0))
        return [tok_spec(width), prev, nxt]

    weights = [lw["g_pre1"], lw["g_post1"], lw["w_gate"], lw["w_attn_o"], lw["pool_w"], lw["pool_scale"],
               lw["w_pool_o"], lw["conv_dw"], lw["conv_b"], lw["conv_ln_g"], lw["conv_ln_b"],
               lw["w_conv_o"], lw["w_out"]]
    in_specs = ([tok_spec(d), tok_spec(ATTN_W)] + halo_specs(POOL_W, POOL_HALO) + halo_specs(CONV_W, CONV_HALO)
                + [pl.BlockSpec((None, 1, mod.shape[-1]), mod_map)] + [_const_spec(w.shape) for w in weights])
    return pl.pallas_call(
        functools.partial(_mixer_kernel, seq_len=ln),
        out_shape=jax.ShapeDtypeStruct(x.shape, F32),
        grid=(bn, n_tiles), in_specs=in_specs, out_specs=tok_spec(d),
        scratch_shapes=[pltpu.VMEM((tile + 2 * POOL_HALO, POOL_W), F32),
                        pltpu.VMEM((tile + 2 * CONV_HALO, CONV_W), F32),
                        pltpu.VMEM((V7X_SUBLANES, tile + 2 * CONV_HALO - V7X_SUBLANES, CONV_W), F32),
                        pltpu.VMEM((tile, CONV_W), BF16)],
        compiler_params=_params(2),
        name="mixer",
    )(x, attn, up, up, up, gc, gc, gc, mod, *weights)


def _ffn_kernel(x_ref, mod_ref, gpre_ref, gpost_ref, w1_ref, w2_ref, out_ref):
    d = D_MODEL
    x = x_ref[...]
    hb = _modulated_rmsnorm(x, gpre_ref[...], mod_ref[:, 3 * d:4 * d], mod_ref[:, 4 * d:5 * d]).astype(BF16)
    y = None
    for lo, hi in FF_CHUNKS:
        a = _dot(hb, w1_ref[:, lo:hi])
        b = _dot(hb, w1_ref[:, D_FF + lo:D_FF + hi])
        part = _dot((a * jax.nn.sigmoid(a) * b).astype(BF16), w2_ref[lo:hi, :])
        y = part if y is None else y + part
    yn = y * lax.rsqrt(jnp.mean(y * y, axis=-1, keepdims=True) + EPS) * gpost_ref[...]
    out_ref[...] = x + mod_ref[:, 5 * d:6 * d] * yn


def _ffn(x, mod, mod_row, g_pre, g_post, w1, w2, *, tile):
    bn, ln, d = x.shape
    if mod_row is None:
        mod_map = lambda b, j: (b, 0, 0)
    else:
        mod_map = lambda b, j: (mod_row, 0, 0)
    tok = pl.BlockSpec((None, tile, d), lambda b, j: (b, j, 0))
    return pl.pallas_call(
        _ffn_kernel,
        out_shape=jax.ShapeDtypeStruct(x.shape, F32),
        grid=(bn, ln // tile),
        in_specs=[tok, pl.BlockSpec((None, 1, mod.shape[-1]), mod_map),
                  _const_spec(g_pre.shape), _const_spec(g_post.shape),
                  _const_spec(w1.shape), _const_spec(w2.shape)],
        out_specs=tok,
        compiler_params=_params(2),
        name="ffn",
    )(x, mod, g_pre, g_post, w1, w2)


def _rope_tables(n_tokens):
    t = jnp.arange(n_tokens)
    row = (t // GRID_W).astype(F32)
    col = (t % GRID_W).astype(F32)
    n_freq = HEAD_DIM // 4
    inv = 1.0 / (ROPE_THETA ** (jnp.arange(n_freq, dtype=F32) / n_freq))
    ang_r = row[:, None] * inv[None, :]
    ang_c = col[:, None] * inv[None, :]
    cos = jnp.concatenate([jnp.cos(ang_r)] * 2 + [jnp.cos(ang_c)] * 2, axis=-1)
    sin = jnp.concatenate([-jnp.sin(ang_r), jnp.sin(ang_r), -jnp.sin(ang_c), jnp.sin(ang_c)], axis=-1)
    return jnp.tile(cos, (1, 2)), jnp.tile(sin, (1, 2))


def kernel(x, c, ctx, c_ctx, w_ada, b_ada, g_pre1, g_post1, g_pre2, g_post2, w_in, q_norm, k_norm,
           w_attn_o, pool_w, pool_scale, w_pool_o, conv_dw, conv_b, conv_ln_g, conv_ln_b, w_conv_o,
           w_out, w_ffn_in, w_ffn_out):
    batch, n_lat, d = x.shape
    n_ctx = ctx.shape[1]
    depth = w_ada.shape[0]
    ctx_row = batch

    cc = jnp.zeros((MOD_ROWS, d), F32).at[:batch].set(c).at[ctx_row].set(c_ctx)
    mod_all = _ada(cc, w_ada, b_ada).reshape(depth, MOD_ROWS, 1, 6 * d)

    rope = _rope_tables(n_lat)
    lane_head = jnp.arange(ATTN_W) // HEAD_DIM
    headmat = jnp.where(lane_head[:, None] == lane_head[None, :], 1.0 / HEAD_DIM, 0.0).astype(BF16)

    lat_tile = 512
    ctx_tile = n_ctx
    e = ctx
    for i in range(depth):
        last = i == depth - 1
        mod = mod_all[i]
        row = lambda a: a[i].reshape(1, -1)
        w_in_b = w_in[i].astype(BF16)
        w_proj = w_in_b[:, :C_END]
        q_gain = jnp.tile(q_norm[i], N_Q_HEADS).reshape(1, -1)
        k_gain = jnp.tile(k_norm[i], N_KV_HEADS).reshape(1, -1)
        lw = {"g_pre1": row(g_pre1), "g_post1": row(g_post1), "w_gate": w_in_b[:, C_END:],
              "w_attn_o": w_attn_o[i].astype(BF16), "pool_w": pool_w[i].astype(BF16),
              "pool_scale": row(pool_scale), "w_pool_o": w_pool_o[i].astype(BF16),
              "conv_dw": conv_dw[i], "conv_b": row(conv_b), "conv_ln_g": row(conv_ln_g),
              "conv_ln_b": row(conv_ln_b), "w_conv_o": w_conv_o[i].astype(BF16),
              "w_out": w_out[i].astype(BF16)}
        w1 = w_ffn_in[i].astype(BF16)
        w2 = w_ffn_out[i].astype(BF16)

        if last:
            kc, vc = _in_proj(e, mod, ctx_row, lw["g_pre1"], w_proj[:, Q_END:V_END], headmat, q_gain, k_gain,
                              None, tile=ctx_tile, kv_only=True)
        else:
            qc, kc, vc, upc, gcc = _in_proj(e, mod, ctx_row, lw["g_pre1"], w_proj, headmat, q_gain, k_gain,
                                            None, tile=ctx_tile)
        qx, kx, vx, upx, gcx = _in_proj(x, mod, None, lw["g_pre1"], w_proj, headmat, q_gain, k_gain,
                                        rope, tile=lat_tile)
        ax = _attention(qx, (kc, kx), (vc, vx), tile=lat_tile)
        x = _mixer(x, ax, upx, gcx, mod, None, lw, tile=lat_tile)
        if not last:
            ac = _attention(qc, (kc,), (vc,), tile=ctx_tile)
            e = _mixer(e, ac, upc, gcc, mod, ctx_row, lw, tile=ctx_tile)

        x = _ffn(x, mod, None, row(g_pre2), row(g_post2), w1, w2, tile=lat_tile)
        if not last:
            e = _ffn(e, mod, ctx_row, row(g_pre2), row(g_post2), w1, w2, tile=ctx_tile)
    return x
```

```python
import functools

import jax
import jax.numpy as jnp
from jax import lax
from jax.experimental import pallas as pl
from jax.experimental.pallas import tpu as pltpu

D_MODEL = 1024
HEAD_DIM = 64
N_Q_HEADS = 8
N_KV_HEADS = 2
Q_GROUP = N_Q_HEADS // N_KV_HEADS
ATTN_W = N_Q_HEADS * HEAD_DIM
KV_W = N_KV_HEADS * HEAD_DIM
GROUP_W = Q_GROUP * HEAD_DIM
POOL_WINDOWS = (2, 4, 8, 16)
POOL_W = D_MODEL // 2
POOL_GROUP_W = POOL_W // len(POOL_WINDOWS)
CONV_W = D_MODEL // 2
CONV_K = 31
GRID_W = 64
ROPE_THETA = 10000.0
EPS = 1e-6
Q_END = ATTN_W
K_END = Q_END + KV_W
V_END = K_END + KV_W
P_END = V_END + POOL_W
C_END = P_END + 2 * CONV_W
D_FF = -(-8 * D_MODEL // (3 * 256)) * 256
FF_CHUNKS = ((0, 1024), (1024, 2048), (2048, D_FF))

V7X_LANES = 128
V7X_SUBLANES = 8
POOL_HALO = 8
CONV_HALO = 16
CONV_ROWS = 32
MOD_ROWS = 24
VMEM_LIMIT = 56 * 1024 * 1024

F32 = jnp.float32
BF16 = jnp.bfloat16


def _const_spec(shape):
    return pl.BlockSpec(shape, lambda *_: (0,) * len(shape), pipeline_mode=pl.Buffered(1))


def _params(n_axes):
    return pltpu.CompilerParams(dimension_semantics=("arbitrary",) * n_axes,
                                vmem_limit_bytes=VMEM_LIMIT)


def _dot(a, b):
    return jnp.dot(a, b, preferred_element_type=F32)


def _modulated_rmsnorm(x, g, shift, scale):
    y = x * lax.rsqrt(jnp.mean(x * x, axis=-1, keepdims=True) + EPS)
    return (y * g) * (1.0 + scale) + shift


def _ada_kernel(c_ref, w_ref, b_ref, o_ref):
    c = c_ref[...]
    s = c * jax.nn.sigmoid(c)
    o_ref[...] = jnp.dot(s, w_ref[...], preferred_element_type=F32,
                         precision=lax.Precision.HIGHEST) + b_ref[...]


def _ada(cc, w_ada, b_ada):
    depth, d, n = w_ada.shape
    bn = 1536
    return pl.pallas_call(
        _ada_kernel,
        out_shape=jax.ShapeDtypeStruct((depth, MOD_ROWS, n), F32),
        grid=(depth, n // bn),
        in_specs=[pl.BlockSpec((MOD_ROWS, d), lambda l, j: (0, 0)),
                  pl.BlockSpec((None, d, bn), lambda l, j: (l, 0, j)),
                  pl.BlockSpec((None, 1, bn), lambda l, j: (l, 0, j))],
        out_specs=pl.BlockSpec((None, MOD_ROWS, bn), lambda l, j: (l, 0, j)),
        compiler_params=_params(2),
        name="adaln",
    )(cc, w_ada, b_ada.reshape(depth, 1, n))


def _head_rms(u, headmat, gain):
    ms = _dot((u * u).astype(BF16), headmat)
    return u * lax.rsqrt(ms + EPS) * gain


def _rope(u, cos, sin_signed):
    lane = lax.broadcasted_iota(jnp.int32, u.shape, 1)
    first_half = (lane % 32) < 16
    partner = jnp.where(first_half,
                        pltpu.roll(u, V7X_LANES - 16, 1),
                        pltpu.roll(u, 16, 1))
    return u * cos + partner * sin_signed


def _replicate_heads(a):
    lane = lax.broadcasted_iota(jnp.int32, a.shape, 1)
    low = lane < HEAD_DIM
    r = pltpu.roll(a, HEAD_DIM, 1)
    return jnp.where(low, a, r), jnp.where(low, r, a)


def _in_proj_kernel(*refs, rope, kv_only):
    it = iter(refs)
    x_ref, mod_ref, gpre_ref, w_ref, hm_ref, qg_ref, kg_ref = (next(it) for _ in range(7))
    cos_ref = sin_ref = None
    if rope:
        cos_ref, sin_ref = next(it), next(it)
    if kv_only:
        k_out, v_out = next(it), next(it)
    else:
        q_out, k_out, v_out, up_out, gc_out = (next(it) for _ in range(5))

    d = D_MODEL
    h = _modulated_rmsnorm(x_ref[...], gpre_ref[...], mod_ref[:, 0:d], mod_ref[:, d:2 * d])
    hb = h.astype(BF16)

    if kv_only:
        ukv = _dot(hb, w_ref[...])
    else:
        ukv = _dot(hb, w_ref[:, Q_END:V_END])
    k = _head_rms(ukv[:, :KV_W], hm_ref[0:KV_W, 0:KV_W], kg_ref[...])
    if rope:
        k = _rope(k, cos_ref[...], sin_ref[...])
    for out, val in ((k_out, k), (v_out, ukv[:, KV_W:])):
        h0, h1 = _replicate_heads(val)
        h0 = h0.astype(BF16)
        h1 = h1.astype(BF16)
        out[:, 0:128] = h0
        out[:, 128:256] = h0
        out[:, 256:384] = h1
        out[:, 384:512] = h1
    if kv_only:
        return

    q = _head_rms(_dot(hb, w_ref[:, 0:Q_END]), hm_ref[...], qg_ref[...])
    for c in range(ATTN_W // V7X_LANES):
        sl = slice(c * V7X_LANES, (c + 1) * V7X_LANES)
        qc = q[:, sl]
        if rope:
            qc = _rope(qc, cos_ref[...], sin_ref[...])
        q_out[:, sl] = (qc * HEAD_DIM ** -0.5).astype(BF16)

    up_out[...] = _dot(hb, w_ref[:, V_END:P_END])
    uc = _dot(hb, w_ref[:, P_END:C_END])
    gc_out[...] = uc[:, :CONV_W] * jax.nn.sigmoid(uc[:, CONV_W:])


def _in_proj(x, mod, mod_row, g_pre, w, headmat, q_gain, k_gain, rope_tabs, *, tile, kv_only=False):
    bn, ln, d = x.shape
    rope = rope_tabs is not None
    if mod_row is None:
        mod_map = lambda b, j: (b, 0, 0)
    else:
        mod_map = lambda b, j: (mod_row, 0, 0)
    in_specs = [pl.BlockSpec((None, tile, d), lambda b, j: (b, j, 0)),
                pl.BlockSpec((None, 1, mod.shape[-1]), mod_map),
                _const_spec(g_pre.shape), _const_spec(w.shape), _const_spec(headmat.shape),
                _const_spec(q_gain.shape), _const_spec(k_gain.shape)]
    args = [x, mod, g_pre, w, headmat, q_gain, k_gain]
    if rope:
        in_specs += [pl.BlockSpec((tile, V7X_LANES), lambda b, j: (j, 0))] * 2
        args += list(rope_tabs)

    def tok_spec(width):
        return pl.BlockSpec((None, tile, width), lambda b, j: (b, j, 0))

    rep = jax.ShapeDtypeStruct((bn, ln, 2 * GROUP_W), BF16)
    if kv_only:
        out_shape = (rep, rep)
        out_specs = (tok_spec(2 * GROUP_W),) * 2
    else:
        out_shape = (jax.ShapeDtypeStruct((bn, ln, ATTN_W), BF16), rep, rep,
                     jax.ShapeDtypeStruct((bn, ln, POOL_W), F32),
                     jax.ShapeDtypeStruct((bn, ln, CONV_W), F32))
        out_specs = (tok_spec(ATTN_W), tok_spec(2 * GROUP_W), tok_spec(2 * GROUP_W),
                     tok_spec(POOL_W), tok_spec(CONV_W))
    return pl.pallas_call(
        functools.partial(_in_proj_kernel, rope=rope, kv_only=kv_only),
        out_shape=out_shape, grid=(bn, ln // tile), in_specs=in_specs, out_specs=out_specs,
        compiler_params=_params(2),
        name="in_proj_kv" if kv_only else "in_proj",
    )(*args)


def _attn_kernel(*refs, n_src, rows):
    q_ref = refs[0]
    k_refs = refs[1:1 + n_src]
    v_refs = refs[1 + n_src:1 + 2 * n_src]
    o_ref = refs[1 + 2 * n_src]
    acc_ref, s_a, s_b, m_a, m_b = refs[2 + 2 * n_src:]
    n_units = (q_ref.shape[0] // rows) * Q_GROUP
    head_of_lane = lax.broadcasted_iota(jnp.int32, (rows, GROUP_W), 1) // HEAD_DIM
    k_widths = [k.shape[0] for k in k_refs]

    def unit_rows(u):
        if isinstance(u, int):
            return pl.ds((u // Q_GROUP) * rows, rows)
        return pl.ds(pl.multiple_of(lax.shift_right_logical(u, 2) * rows, rows), rows)

    def scores(u, s_ref, m_ref):
        q = q_ref[unit_rows(u), :]
        qg = jnp.where(head_of_lane == (u & (Q_GROUP - 1)), q, jnp.zeros_like(q))
        m = None
        col = 0
        for k, width in zip(k_refs, k_widths):
            s = lax.dot_general(qg, k[...], (((1,), (1,)), ((), ())), preferred_element_type=F32)
            s_ref[:, col:col + width] = s
            col += width
            mk = jnp.max(s, axis=-1, keepdims=True)
            m = mk if m is None else jnp.maximum(m, mk)
        m_ref[...] = m

    def softmax_pv(u, s_ref, m_ref):
        m = m_ref[...]
        denom = None
        og = None
        col = 0
        for v, width in zip(v_refs, k_widths):
            p = jnp.exp(s_ref[:, col:col + width] - m)
            col += width
            dk = jnp.sum(p, axis=-1, keepdims=True)
            ok = _dot(p.astype(BF16), v[...])
            denom = dk if denom is None else denom + dk
            og = ok if og is None else og + ok
        r = unit_rows(u)
        acc_ref[r, :] = jnp.where(head_of_lane == (u & (Q_GROUP - 1)), og / denom, acc_ref[r, :])

    acc_ref[...] = jnp.zeros_like(acc_ref)
    scores(0, s_a, m_a)

    def pair(i, carry):
        u = 2 * i
        scores(u + 1, s_b, m_b)
        softmax_pv(u, s_a, m_a)
        scores(u + 2, s_a, m_a)
        softmax_pv(u + 1, s_b, m_b)
        return carry

    lax.fori_loop(0, n_units // 2 - 1, pair, 0)
    scores(n_units - 1, s_b, m_b)
    softmax_pv(n_units - 2, s_a, m_a)
    softmax_pv(n_units - 1, s_b, m_b)
    o_ref[...] = acc_ref[...].astype(o_ref.dtype)


def _attention(q, k_srcs, v_srcs, *, rows):
    bn, ln, _ = q.shape
    n_src = len(k_srcs)
    n_keys = sum(k.shape[1] for k in k_srcs)
    q_spec = pl.BlockSpec((None, ln, GROUP_W), lambda b, h: (b, 0, h))
    kv_specs = [pl.BlockSpec((None, k.shape[1], GROUP_W), lambda b, h: (b, 0, h))
                for k in tuple(k_srcs) + tuple(v_srcs)]
    return pl.pallas_call(
        functools.partial(_attn_kernel, n_src=n_src, rows=rows),
        out_shape=jax.ShapeDtypeStruct((bn, ln, ATTN_W), BF16),
        grid=(bn, N_KV_HEADS),
        in_specs=[q_spec] + kv_specs, out_specs=q_spec,
        scratch_shapes=[pltpu.VMEM((ln, GROUP_W), F32),
                        pltpu.VMEM((rows, n_keys), F32), pltpu.VMEM((rows, n_keys), F32),
                        pltpu.VMEM((rows, 1), F32), pltpu.VMEM((rows, 1), F32)],
        compiler_params=_params(2),
        name="attention",
    )(q, *k_srcs, *v_srcs)


def _mixer_kernel(x_ref, o_ref, up_ref, up_prev_ref, up_next_ref, gc_ref, gc_prev_ref, gc_next_ref,
                  mod_ref, gpre_ref, gpost_ref, wgate_ref, wao_ref, poolw_ref, pscale_ref, wpo_ref,
                  cdw_ref, cb_ref, lng_ref, lnb_ref, wco_ref, wout_ref,
                  out_ref, ppad_ref, gpad_ref, gsh_ref, ybuf_ref, *, seq_len):
    tile = x_ref.shape[0]
    d = D_MODEL
    j = pl.program_id(1)
    first = j == 0
    last = j == pl.num_programs(1) - 1

    x = x_ref[...]
    hb = _modulated_rmsnorm(x, gpre_ref[...], mod_ref[:, 0:d], mod_ref[:, d:2 * d]).astype(BF16)

    merged = jax.nn.sigmoid(_dot(hb, wgate_ref[:, 0:d])) * _dot(o_ref[...], wao_ref[...])

    ppad_ref[0:POOL_HALO, :] = jnp.where(first, 0.0, up_prev_ref[...])
    ppad_ref[POOL_HALO:POOL_HALO + tile, :] = up_ref[...]
    ppad_ref[POOL_HALO + tile:, :] = jnp.where(last, 0.0, up_next_ref[...])
    t_abs = j * tile + lax.broadcasted_iota(jnp.int32, (tile, POOL_GROUP_W), 0)
    pooled = []
    for gi, w in enumerate(POOL_WINDOWS):
        cols = slice(gi * POOL_GROUP_W, (gi + 1) * POOL_GROUP_W)
        tot = ppad_ref[POOL_HALO - w // 2:POOL_HALO - w // 2 + tile, cols]
        for off in range(-w // 2 + 1, w // 2):
            tot = tot + ppad_ref[POOL_HALO + off:POOL_HALO + off + tile, cols]
        cnt = jnp.minimum(t_abs + w // 2, seq_len) - jnp.maximum(t_abs - w // 2, 0)
        dg = tot / cnt.astype(F32) - ppad_ref[POOL_HALO:POOL_HALO + tile, cols]
        pooled.append(_dot(dg.astype(BF16), poolw_ref[gi]))
    yb = jnp.concatenate(pooled, axis=-1) * pscale_ref[...]
    merged = merged + jax.nn.sigmoid(_dot(hb, wgate_ref[:, d:2 * d])) * _dot(yb.astype(BF16), wpo_ref[...])

    gpad_ref[0:CONV_HALO, :] = jnp.where(first, 0.0, gc_prev_ref[...])
    gpad_ref[CONV_HALO:CONV_HALO + tile, :] = gc_ref[...]
    gpad_ref[CONV_HALO + tile:, :] = jnp.where(last, 0.0, gc_next_ref[...])
    tap0 = CONV_HALO - CONV_K // 2
    shifted_rows = gsh_ref.shape[1]
    for r in range(V7X_SUBLANES):
        gsh_ref[r] = gpad_ref[r:r + shifted_rows, :]

    def conv_chunk(c, carry):
        r0 = pl.multiple_of(c * CONV_ROWS, CONV_ROWS)
        acc = jnp.broadcast_to(cb_ref[...], (CONV_ROWS, CONV_W))
        for k in range(CONV_K):
            phase = (tap0 + k) % V7X_SUBLANES
            base = (tap0 + k) - phase
            acc = acc + cdw_ref[k:k + 1, :] * gsh_ref[phase, pl.ds(r0 + base, CONV_ROWS), :]
        mu = jnp.mean(acc, axis=-1, keepdims=True)
        cen = acc - mu
        var = jnp.mean(cen * cen, axis=-1, keepdims=True)
        y = cen * lax.rsqrt(var + EPS) * lng_ref[...] + lnb_ref[...]
        ybuf_ref[pl.ds(r0, CONV_ROWS), :] = (y * jax.nn.sigmoid(y)).astype(BF16)
        return carry

    lax.fori_loop(0, tile // CONV_ROWS, conv_chunk, 0)
    merged = merged + jax.nn.sigmoid(_dot(hb, wgate_ref[:, 2 * d:3 * d])) * _dot(ybuf_ref[...], wco_ref[...])

    y = _dot(merged.astype(BF16), wout_ref[...])
    yn = y * lax.rsqrt(jnp.mean(y * y, axis=-1, keepdims=True) + EPS) * gpost_ref[...]
    out_ref[...] = x + mod_ref[:, 2 * d:3 * d] * yn


def _mixer(x, attn, up, gc, mod, mod_row, lw, *, tile):
    bn, ln, d = x.shape
    n_tiles = ln // tile
    if mod_row is None:
        mod_map = lambda b, j: (b, 0, 0)
    else:
        mod_map = lambda b, j: (mod_row, 0, 0)

    def tok_spec(width):
        return pl.BlockSpec((None, tile, width), lambda b, j: (b, j, 0))

    def halo_specs(width, halo):
        per_tile = tile // halo
        n_blocks = ln // halo
        prev = pl.BlockSpec((None, halo, width), lambda b, j: (b, jnp.maximum(j * per_tile - 1, 0), 0))
        nxt = pl.BlockSpec((None, halo, width),
                           lambda b, j: (b, jnp.minimum((j + 1) * per_tile, n_blocks - 1), 0))
        return [tok_spec(width), prev, nxt]

    weights = [lw["g_pre1"], lw["g_post1"], lw["w_gate"], lw["w_attn_o"], lw["pool_w"], lw["pool_scale"],
               lw["w_pool_o"], lw["conv_dw"], lw["conv_b"], lw["conv_ln_g"], lw["conv_ln_b"],
               lw["w_conv_o"], lw["w_out"]]
    in_specs = ([tok_spec(d), tok_spec(ATTN_W)] + halo_specs(POOL_W, POOL_HALO) + halo_specs(CONV_W, CONV_HALO)
                + [pl.BlockSpec((None, 1, mod.shape[-1]), mod_map)] + [_const_spec(w.shape) for w in weights])
    return pl.pallas_call(
        functools.partial(_mixer_kernel, seq_len=ln),
        out_shape=jax.ShapeDtypeStruct(x.shape, F32),
        grid=(bn, n_tiles), in_specs=in_specs, out_specs=tok_spec(d),
        scratch_shapes=[pltpu.VMEM((tile + 2 * POOL_HALO, POOL_W), F32),
                        pltpu.VMEM((tile + 2 * CONV_HALO, CONV_W), F32),
                        pltpu.VMEM((V7X_SUBLANES, tile + 2 * CONV_HALO - V7X_SUBLANES, CONV_W), F32),
                        pltpu.VMEM((tile, CONV_W), BF16)],
        compiler_params=_params(2),
        name="mixer",
    )(x, attn, up, up, up, gc, gc, gc, mod, *weights)


def _ffn_kernel(x_ref, mod_ref, gpre_ref, gpost_ref, w1_ref, w2_ref, out_ref):
    d = D_MODEL
    x = x_ref[...]
    hb = _modulated_rmsnorm(x, gpre_ref[...], mod_ref[:, 3 * d:4 * d], mod_ref[:, 4 * d:5 * d]).astype(BF16)
    y = None
    for lo, hi in FF_CHUNKS:
        a = _dot(hb, w1_ref[:, lo:hi])
        b = _dot(hb, w1_ref[:, D_FF + lo:D_FF + hi])
        part = _dot((a * jax.nn.sigmoid(a) * b).astype(BF16), w2_ref[lo:hi, :])
        y = part if y is None else y + part
    yn = y * lax.rsqrt(jnp.mean(y * y, axis=-1, keepdims=True) + EPS) * gpost_ref[...]
    out_ref[...] = x + mod_ref[:, 5 * d:6 * d] * yn


def _ffn(x, mod, mod_row, g_pre, g_post, w1, w2, *, tile):
    bn, ln, d = x.shape
    if mod_row is None:
        mod_map = lambda b, j: (b, 0, 0)
    else:
        mod_map = lambda b, j: (mod_row, 0, 0)
    tok = pl.BlockSpec((None, tile, d), lambda b, j: (b, j, 0))
    return pl.pallas_call(
        _ffn_kernel,
        out_shape=jax.ShapeDtypeStruct(x.shape, F32),
        grid=(bn, ln // tile),
        in_specs=[tok, pl.BlockSpec((None, 1, mod.shape[-1]), mod_map),
                  _const_spec(g_pre.shape), _const_spec(g_post.shape),
                  _const_spec(w1.shape), _const_spec(w2.shape)],
        out_specs=tok,
        compiler_params=_params(2),
        name="ffn",
    )(x, mod, g_pre, g_post, w1, w2)


def _rope_tables(n_tokens):
    t = jnp.arange(n_tokens)
    row = (t // GRID_W).astype(F32)
    col = (t % GRID_W).astype(F32)
    n_freq = HEAD_DIM // 4
    inv = 1.0 / (ROPE_THETA ** (jnp.arange(n_freq, dtype=F32) / n_freq))
    ang_r = row[:, None] * inv[None, :]
    ang_c = col[:, None] * inv[None, :]
    cos = jnp.concatenate([jnp.cos(ang_r)] * 2 + [jnp.cos(ang_c)] * 2, axis=-1)
    sin = jnp.concatenate([-jnp.sin(ang_r), jnp.sin(ang_r), -jnp.sin(ang_c), jnp.sin(ang_c)], axis=-1)
    return jnp.tile(cos, (1, 2)), jnp.tile(sin, (1, 2))


def kernel(x, c, ctx, c_ctx, w_ada, b_ada, g_pre1, g_post1, g_pre2, g_post2, w_in, q_norm, k_norm,
           w_attn_o, pool_w, pool_scale, w_pool_o, conv_dw, conv_b, conv_ln_g, conv_ln_b, w_conv_o,
           w_out, w_ffn_in, w_ffn_out):
    batch, n_lat, d = x.shape
    n_ctx = ctx.shape[1]
    depth = w_ada.shape[0]
    ctx_row = batch

    cc = jnp.zeros((MOD_ROWS, d), F32).at[:batch].set(c).at[ctx_row].set(c_ctx)
    mod_all = _ada(cc, w_ada, b_ada).reshape(depth, MOD_ROWS, 1, 6 * d)

    rope = _rope_tables(n_lat)
    lane_head = jnp.arange(ATTN_W) // HEAD_DIM
    headmat = jnp.where(lane_head[:, None] == lane_head[None, :], 1.0 / HEAD_DIM, 0.0).astype(BF16)

    lat_tile = 512
    ctx_tile = n_ctx
    e = ctx
    for i in range(depth):
        last = i == depth - 1
        mod = mod_all[i]
        row = lambda a: a[i].reshape(1, -1)
        w_in_b = w_in[i].astype(BF16)
        w_proj = w_in_b[:, :C_END]
        q_gain = jnp.tile(q_norm[i], N_Q_HEADS).reshape(1, -1)
        k_gain = jnp.tile(k_norm[i], N_KV_HEADS).reshape(1, -1)
        lw = {"g_pre1": row(g_pre1), "g_post1": row(g_post1), "w_gate": w_in_b[:, C_END:],
              "w_attn_o": w_attn_o[i].astype(BF16), "pool_w": pool_w[i].astype(BF16),
              "pool_scale": row(pool_scale), "w_pool_o": w_pool_o[i].astype(BF16),
              "conv_dw": conv_dw[i], "conv_b": row(conv_b), "conv_ln_g": row(conv_ln_g),
              "conv_ln_b": row(conv_ln_b), "w_conv_o": w_conv_o[i].astype(BF16),
              "w_out": w_out[i].astype(BF16)}
        w1 = w_ffn_in[i].astype(BF16)
        w2 = w_ffn_out[i].astype(BF16)

        if last:
            kc, vc = _in_proj(e, mod, ctx_row, lw["g_pre1"], w_proj[:, Q_END:V_END], headmat, q_gain, k_gain,
                              None, tile=ctx_tile, kv_only=True)
        else:
            qc, kc, vc, upc, gcc = _in_proj(e, mod, ctx_row, lw["g_pre1"], w_proj, headmat, q_gain, k_gain,
                                            None, tile=ctx_tile)
        qx, kx, vx, upx, gcx = _in_proj(x, mod, None, lw["g_pre1"], w_proj, headmat, q_gain, k_gain,
                                        rope, tile=lat_tile)
        ax = _attention(qx, (kc, kx), (vc, vx), rows=512)
        x = _mixer(x, ax, upx, gcx, mod, None, lw, tile=lat_tile)
        if not last:
            ac = _attention(qc, (kc,), (vc,), rows=ctx_tile)
            e = _mixer(e, ac, upc, gcc, mod, ctx_row, lw, tile=ctx_tile)

        x = _ffn(x, mod, None, row(g_pre2), row(g_post2), w1, w2, tile=lat_tile)
        if not last:
            e = _ffn(e, mod, ctx_row, row(g_pre2), row(g_post2), w1, w2, tile=ctx_tile)
    return x
```

```python
import functools

import jax
import jax.numpy as jnp
from jax import lax
from jax.experimental import pallas as pl
from jax.experimental.pallas import tpu as pltpu

D_MODEL = 1024
HEAD_DIM = 64
N_Q_HEADS = 8
N_KV_HEADS = 2
Q_GROUP = N_Q_HEADS // N_KV_HEADS
ATTN_W = N_Q_HEADS * HEAD_DIM
KV_W = N_KV_HEADS * HEAD_DIM
GROUP_W = Q_GROUP * HEAD_DIM
POOL_WINDOWS = (2, 4, 8, 16)
POOL_W = D_MODEL // 2
POOL_GROUP_W = POOL_W // len(POOL_WINDOWS)
CONV_W = D_MODEL // 2
CONV_K = 31
GRID_W = 64
ROPE_THETA = 10000.0
EPS = 1e-6
LOG2_E = 1.4426950408889634
Q_END = ATTN_W
K_END = Q_END + KV_W
V_END = K_END + KV_W
P_END = V_END + POOL_W
C_END = P_END + 2 * CONV_W
D_FF = -(-8 * D_MODEL // (3 * 256)) * 256
FF_CHUNKS = ((0, 1024), (1024, 2048), (2048, D_FF))

V7X_LANES = 128
V7X_SUBLANES = 8
POOL_HALO = 8
POOL_SLAB_PAD = 128
CONV_HALO = 16
CONV_ROWS = 32
MOD_ROWS = 24
VMEM_LIMIT = 56 * 1024 * 1024

F32 = jnp.float32
BF16 = jnp.bfloat16


def _const_spec(shape):
    return pl.BlockSpec(shape, lambda *_: (0,) * len(shape), pipeline_mode=pl.Buffered(1))


def _params(n_axes):
    return pltpu.CompilerParams(dimension_semantics=("arbitrary",) * n_axes,
                                vmem_limit_bytes=VMEM_LIMIT)


def _dot(a, b):
    return jnp.dot(a, b, preferred_element_type=F32)


def _sigmoid(z):
    return 0.5 * jnp.tanh(0.5 * z) + 0.5


def _modulated_rmsnorm(x, g, shift, scale):
    y = x * lax.rsqrt(jnp.mean(x * x, axis=-1, keepdims=True) + EPS)
    return (y * g) * (1.0 + scale) + shift


def _ada_kernel(c_ref, w_ref, b_ref, o_ref):
    c = c_ref[...]
    s = c * jax.nn.sigmoid(c)
    o_ref[...] = jnp.dot(s, w_ref[...], preferred_element_type=F32,
                         precision=lax.Precision.HIGHEST) + b_ref[...]


def _ada(cc, w_ada, b_ada):
    depth, d, n = w_ada.shape
    bn = 1536
    return pl.pallas_call(
        _ada_kernel,
        out_shape=jax.ShapeDtypeStruct((depth, MOD_ROWS, n), F32),
        grid=(depth, n // bn),
        in_specs=[pl.BlockSpec((MOD_ROWS, d), lambda l, j: (0, 0)),
                  pl.BlockSpec((None, d, bn), lambda l, j: (l, 0, j)),
                  pl.BlockSpec((None, 1, bn), lambda l, j: (l, 0, j))],
        out_specs=pl.BlockSpec((None, MOD_ROWS, bn), lambda l, j: (l, 0, j)),
        compiler_params=_params(2),
        name="adaln",
    )(cc, w_ada, b_ada.reshape(depth, 1, n))


def _head_rms(u, headmat, gain):
    ms = _dot((u * u).astype(BF16), headmat)
    return u * lax.rsqrt(ms + EPS) * gain


def _rope(u, cos, sin_signed):
    lane = lax.broadcasted_iota(jnp.int32, u.shape, 1)
    first_half = (lane % 32) < 16
    partner = jnp.where(first_half,
                        pltpu.roll(u, V7X_LANES - 16, 1),
                        pltpu.roll(u, 16, 1))
    return u * cos + partner * sin_signed


def _replicate_heads(a):
    lane = lax.broadcasted_iota(jnp.int32, a.shape, 1)
    low = lane < HEAD_DIM
    r = pltpu.roll(a, HEAD_DIM, 1)
    return jnp.where(low, a, r), jnp.where(low, r, a)


def _in_proj_kernel(*refs, rope, kv_only):
    it = iter(refs)
    x_ref, mod_ref, gpre_ref, w_ref, hm_ref, qg_ref, kg_ref = (next(it) for _ in range(7))
    cos_ref = sin_ref = None
    if rope:
        cos_ref, sin_ref = next(it), next(it)
    if kv_only:
        k_out, v_out = next(it), next(it)
    else:
        q_out, k_out, v_out, up_out, gc_out = (next(it) for _ in range(5))

    d = D_MODEL
    h = _modulated_rmsnorm(x_ref[...], gpre_ref[...], mod_ref[:, 0:d], mod_ref[:, d:2 * d])
    hb = h.astype(BF16)

    if kv_only:
        ukv = _dot(hb, w_ref[...])
    else:
        ukv = _dot(hb, w_ref[:, Q_END:V_END])
    k = _head_rms(ukv[:, :KV_W], hm_ref[0:KV_W, 0:KV_W], kg_ref[...])
    if rope:
        k = _rope(k, cos_ref[...], sin_ref[...])
    for out, val in ((k_out, k), (v_out, ukv[:, KV_W:])):
        h0, h1 = _replicate_heads(val)
        h0 = h0.astype(BF16)
        h1 = h1.astype(BF16)
        out[:, 0:128] = h0
        out[:, 128:256] = h0
        out[:, 256:384] = h1
        out[:, 384:512] = h1
    if kv_only:
        return

    q = _head_rms(_dot(hb, w_ref[:, 0:Q_END]), hm_ref[...], qg_ref[...])
    for c in range(ATTN_W // V7X_LANES):
        sl = slice(c * V7X_LANES, (c + 1) * V7X_LANES)
        qc = q[:, sl]
        if rope:
            qc = _rope(qc, cos_ref[...], sin_ref[...])
        q_out[:, sl] = (qc * (HEAD_DIM ** -0.5 * LOG2_E)).astype(BF16)

    up_out[...] = _dot(hb, w_ref[:, V_END:P_END])
    uc = _dot(hb, w_ref[:, P_END:C_END])
    gc_out[...] = uc[:, :CONV_W] * _sigmoid(uc[:, CONV_W:])


def _in_proj(x, mod, mod_row, g_pre, w, headmat, q_gain, k_gain, rope_tabs, *, tile, kv_only=False):
    bn, ln, d = x.shape
    rope = rope_tabs is not None
    if mod_row is None:
        mod_map = lambda b, j: (b, 0, 0)
    else:
        mod_map = lambda b, j: (mod_row, 0, 0)
    in_specs = [pl.BlockSpec((None, tile, d), lambda b, j: (b, j, 0)),
                pl.BlockSpec((None, 1, mod.shape[-1]), mod_map),
                _const_spec(g_pre.shape), _const_spec(w.shape), _const_spec(headmat.shape),
                _const_spec(q_gain.shape), _const_spec(k_gain.shape)]
    args = [x, mod, g_pre, w, headmat, q_gain, k_gain]
    if rope:
        in_specs += [pl.BlockSpec((tile, V7X_LANES), lambda b, j: (j, 0))] * 2
        args += list(rope_tabs)

    def tok_spec(width):
        return pl.BlockSpec((None, tile, width), lambda b, j: (b, j, 0))

    rep = jax.ShapeDtypeStruct((bn, ln, 2 * GROUP_W), BF16)
    if kv_only:
        out_shape = (rep, rep)
        out_specs = (tok_spec(2 * GROUP_W),) * 2
    else:
        out_shape = (jax.ShapeDtypeStruct((bn, ln, ATTN_W), BF16), rep, rep,
                     jax.ShapeDtypeStruct((bn, ln, POOL_W), F32),
                     jax.ShapeDtypeStruct((bn, ln, CONV_W), F32))
        out_specs = (tok_spec(ATTN_W), tok_spec(2 * GROUP_W), tok_spec(2 * GROUP_W),
                     tok_spec(POOL_W), tok_spec(CONV_W))
    return pl.pallas_call(
        functools.partial(_in_proj_kernel, rope=rope, kv_only=kv_only),
        out_shape=out_shape, grid=(bn, ln // tile), in_specs=in_specs, out_specs=out_specs,
        compiler_params=_params(2),
        name="in_proj_kv" if kv_only else "in_proj",
    )(*args)


def _attn_kernel(*refs, n_src, rows):
    q_ref = refs[0]
    k_refs = refs[1:1 + n_src]
    v_refs = refs[1 + n_src:1 + 2 * n_src]
    o_ref = refs[1 + 2 * n_src]
    acc_ref, s_a, s_b, m_a, m_b = refs[2 + 2 * n_src:]
    n_units = (q_ref.shape[0] // rows) * Q_GROUP
    head_of_lane = lax.broadcasted_iota(jnp.int32, (rows, GROUP_W), 1) // HEAD_DIM
    k_widths = [k.shape[0] for k in k_refs]

    def unit_rows(u):
        if isinstance(u, int):
            return pl.ds((u // Q_GROUP) * rows, rows)
        return pl.ds(pl.multiple_of(lax.shift_right_logical(u, 2) * rows, rows), rows)

    def scores(u, s_ref, m_ref):
        q = q_ref[unit_rows(u), :]
        qg = jnp.where(head_of_lane == (u & (Q_GROUP - 1)), q, jnp.zeros_like(q))
        m = None
        col = 0
        for k, width in zip(k_refs, k_widths):
            s = lax.dot_general(qg, k[...], (((1,), (1,)), ((), ())), preferred_element_type=F32)
            s_ref[:, col:col + width] = s
            col += width
            mk = jnp.max(s, axis=-1, keepdims=True)
            m = mk if m is None else jnp.maximum(m, mk)
        m_ref[...] = m

    def softmax_pv(u, s_ref, m_ref):
        m = m_ref[...]
        denom = None
        og = None
        col = 0
        for v, width in zip(v_refs, k_widths):
            p = jnp.exp2(s_ref[:, col:col + width] - m)
            col += width
            dk = jnp.sum(p, axis=-1, keepdims=True)
            ok = _dot(p.astype(BF16), v[...])
            denom = dk if denom is None else denom + dk
            og = ok if og is None else og + ok
        r = unit_rows(u)
        acc_ref[r, :] = jnp.where(head_of_lane == (u & (Q_GROUP - 1)), og / denom, acc_ref[r, :])

    acc_ref[...] = jnp.zeros_like(acc_ref)
    scores(0, s_a, m_a)

    def pair(i, carry):
        u = 2 * i
        scores(u + 1, s_b, m_b)
        softmax_pv(u, s_a, m_a)
        scores(u + 2, s_a, m_a)
        softmax_pv(u + 1, s_b, m_b)
        return carry

    lax.fori_loop(0, n_units // 2 - 1, pair, 0)
    scores(n_units - 1, s_b, m_b)
    softmax_pv(n_units - 2, s_a, m_a)
    softmax_pv(n_units - 1, s_b, m_b)
    o_ref[...] = acc_ref[...].astype(o_ref.dtype)


def _attention(q, k_srcs, v_srcs, *, rows):
    bn, ln, _ = q.shape
    n_src = len(k_srcs)
    n_keys = sum(k.shape[1] for k in k_srcs)
    q_spec = pl.BlockSpec((None, ln, GROUP_W), lambda b, h: (b, 0, h))
    kv_specs = [pl.BlockSpec((None, k.shape[1], GROUP_W), lambda b, h: (b, 0, h))
                for k in tuple(k_srcs) + tuple(v_srcs)]
    return pl.pallas_call(
        functools.partial(_attn_kernel, n_src=n_src, rows=rows),
        out_shape=jax.ShapeDtypeStruct((bn, ln, ATTN_W), BF16),
        grid=(bn, N_KV_HEADS),
        in_specs=[q_spec] + kv_specs, out_specs=q_spec,
        scratch_shapes=[pltpu.VMEM((ln, GROUP_W), F32),
                        pltpu.VMEM((rows, n_keys), F32), pltpu.VMEM((rows, n_keys), F32),
                        pltpu.VMEM((rows, 1), F32), pltpu.VMEM((rows, 1), F32)],
        compiler_params=_params(2),
        name="attention",
    )(q, *k_srcs, *v_srcs)


def _mixer_kernel(x_ref, o_ref, up_ref, up_prev_ref, up_next_ref, gc_ref, gc_prev_ref, gc_next_ref,
                  icnt_ref, mod_ref, band_ref, gpre_ref, gpost_ref, wgate_ref, wao_ref, poolw_ref, pscale_ref,
                  wpo_ref, cdw_ref, cb_ref, lng_ref, lnb_ref, wco_ref, wout_ref,
                  out_ref, ppad_ref, gpad_ref, gsh_ref, cacc_ref):
    tile = x_ref.shape[0]
    d = D_MODEL
    j = pl.program_id(1)
    first = j == 0
    last = j == pl.num_programs(1) - 1

    x = x_ref[...]
    hb = _modulated_rmsnorm(x, gpre_ref[...], mod_ref[:, 0:d], mod_ref[:, d:2 * d]).astype(BF16)

    merged = _sigmoid(_dot(hb, wgate_ref[:, 0:d])) * _dot(o_ref[...], wao_ref[...])

    ppad_ref[0:POOL_HALO, :] = jnp.where(first, 0.0, up_prev_ref[...])
    ppad_ref[POOL_HALO:POOL_HALO + tile, :] = up_ref[...]
    ppad_ref[POOL_HALO + tile:2 * POOL_HALO + tile, :] = jnp.where(last, 0.0, up_next_ref[...])
    ppad_ref[2 * POOL_HALO + tile:, :] = jnp.zeros((ppad_ref.shape[0] - 2 * POOL_HALO - tile, POOL_W), F32)
    pb = ppad_ref[...].astype(BF16)
    pooled = []
    for gi in range(len(POOL_WINDOWS)):
        cols = slice(gi * POOL_GROUP_W, (gi + 1) * POOL_GROUP_W)
        dg = _dot(band_ref[gi], pb[:, cols]) * icnt_ref[:, cols] - up_ref[:, cols]
        pooled.append(_dot(dg.astype(BF16), poolw_ref[gi]))
    yb = jnp.concatenate(pooled, axis=-1) * pscale_ref[...]
    merged = merged + _sigmoid(_dot(hb, wgate_ref[:, d:2 * d])) * _dot(yb.astype(BF16), wpo_ref[...])

    gpad_ref[0:CONV_HALO, :] = jnp.where(first, 0.0, gc_prev_ref[...])
    gpad_ref[CONV_HALO:CONV_HALO + tile, :] = gc_ref[...]
    gpad_ref[CONV_HALO + tile:, :] = jnp.where(last, 0.0, gc_next_ref[...])
    tap0 = CONV_HALO - CONV_K // 2
    shifted_rows = gsh_ref.shape[1]
    for r in range(V7X_SUBLANES):
        gsh_ref[r] = gpad_ref[r:r + shifted_rows, :]
    groups = CONV_ROWS // V7X_SUBLANES
    for r0 in range(0, tile, CONV_ROWS):
        acc = jnp.broadcast_to(cb_ref[...], (groups, V7X_SUBLANES, CONV_W))
        for k in range(CONV_K):
            phase = (tap0 + k) % V7X_SUBLANES
            base = (tap0 + k) - phase
            g = gsh_ref[phase, r0 + base:r0 + base + CONV_ROWS, :].reshape(groups, V7X_SUBLANES, CONV_W)
            acc = acc + cdw_ref[k * V7X_SUBLANES:(k + 1) * V7X_SUBLANES, :][None] * g
        cacc_ref[r0:r0 + CONV_ROWS, :] = acc.reshape(CONV_ROWS, CONV_W)
    yc = cacc_ref[...]
    cen = yc - jnp.mean(yc, axis=-1, keepdims=True)
    var = jnp.mean(cen * cen, axis=-1, keepdims=True)
    yc = cen * lax.rsqrt(var + EPS) * lng_ref[...] + lnb_ref[...]
    yc = (yc * _sigmoid(yc)).astype(BF16)
    merged = merged + _sigmoid(_dot(hb, wgate_ref[:, 2 * d:3 * d])) * _dot(yc, wco_ref[...])

    y = _dot(merged.astype(BF16), wout_ref[...])
    yn = y * lax.rsqrt(jnp.mean(y * y, axis=-1, keepdims=True) + EPS) * gpost_ref[...]
    out_ref[...] = x + mod_ref[:, 2 * d:3 * d] * yn


def _pool_tables(n_tokens, tile):
    slab = tile + POOL_SLAB_PAD
    t = jnp.arange(tile)[:, None]
    i = jnp.arange(slab)[None, :] - POOL_HALO
    band = jnp.stack([(i >= t - w // 2) & (i < t + w // 2) for w in POOL_WINDOWS]).astype(BF16)
    pos = jnp.arange(n_tokens)
    inv = [1.0 / (jnp.minimum(pos + w // 2, n_tokens) - jnp.maximum(pos - w // 2, 0)).astype(F32)
           for w in POOL_WINDOWS]
    return band, jnp.repeat(jnp.stack(inv, axis=-1), POOL_GROUP_W, axis=-1)


def _mixer(x, attn, up, gc, mod, mod_row, lw, *, tile):
    bn, ln, d = x.shape
    n_tiles = ln // tile
    if mod_row is None:
        mod_map = lambda b, j: (b, 0, 0)
    else:
        mod_map = lambda b, j: (mod_row, 0, 0)

    def tok_spec(width):
        return pl.BlockSpec((None, tile, width), lambda b, j: (b, j, 0))

    def halo_specs(width, halo):
        per_tile = tile // halo
        n_blocks = ln // halo
        prev = pl.BlockSpec((None, halo, width), lambda b, j: (b, jnp.maximum(j * per_tile - 1, 0), 0))
        nxt = pl.BlockSpec((None, halo, width),
                           lambda b, j: (b, jnp.minimum((j + 1) * per_tile, n_blocks - 1), 0))
        return [tok_spec(width), prev, nxt]

    band, inv_cnt = _pool_tables(ln, tile)
    weights = [band, lw["g_pre1"], lw["g_post1"], lw["w_gate"], lw["w_attn_o"], lw["pool_w"], lw["pool_scale"],
               lw["w_pool_o"], lw["conv_dw"], lw["conv_b"], lw["conv_ln_g"], lw["conv_ln_b"],
               lw["w_conv_o"], lw["w_out"]]
    in_specs = ([tok_spec(d), tok_spec(ATTN_W)] + halo_specs(POOL_W, POOL_HALO) + halo_specs(CONV_W, CONV_HALO)
                + [pl.BlockSpec((tile, POOL_W), lambda b, j: (j, 0)),
                   pl.BlockSpec((None, 1, mod.shape[-1]), mod_map)] + [_const_spec(w.shape) for w in weights])
    return pl.pallas_call(
        _mixer_kernel,
        out_shape=jax.ShapeDtypeStruct(x.shape, F32),
        grid=(bn, n_tiles), in_specs=in_specs, out_specs=tok_spec(d),
        scratch_shapes=[pltpu.VMEM((band.shape[-1], POOL_W), F32),
                        pltpu.VMEM((tile + 2 * CONV_HALO, CONV_W), F32),
                        pltpu.VMEM((V7X_SUBLANES, tile + 2 * CONV_HALO - V7X_SUBLANES, CONV_W), F32),
                        pltpu.VMEM((tile, CONV_W), F32)],
        compiler_params=_params(2),
        name="mixer",
    )(x, attn, up, up, up, gc, gc, gc, inv_cnt, mod, *weights)


def _ffn_kernel(x_ref, mod_ref, gpre_ref, gpost_ref, w1_ref, w2_ref, out_ref):
    d = D_MODEL
    x = x_ref[...]
    hb = _modulated_rmsnorm(x, gpre_ref[...], mod_ref[:, 3 * d:4 * d], mod_ref[:, 4 * d:5 * d]).astype(BF16)
    y = None
    for lo, hi in FF_CHUNKS:
        a = _dot(hb, w1_ref[:, lo:hi])
        b = _dot(hb, w1_ref[:, D_FF + lo:D_FF + hi])
        part = _dot((a * _sigmoid(a) * b).astype(BF16), w2_ref[lo:hi, :])
        y = part if y is None else y + part
    yn = y * lax.rsqrt(jnp.mean(y * y, axis=-1, keepdims=True) + EPS) * gpost_ref[...]
    out_ref[...] = x + mod_ref[:, 5 * d:6 * d] * yn


def _ffn(x, mod, mod_row, g_pre, g_post, w1, w2, *, tile):
    bn, ln, d = x.shape
    if mod_row is None:
        mod_map = lambda b, j: (b, 0, 0)
    else:
        mod_map = lambda b, j: (mod_row, 0, 0)
    tok = pl.BlockSpec((None, tile, d), lambda b, j: (b, j, 0))
    return pl.pallas_call(
        _ffn_kernel,
        out_shape=jax.ShapeDtypeStruct(x.shape, F32),
        grid=(bn, ln // tile),
        in_specs=[tok, pl.BlockSpec((None, 1, mod.shape[-1]), mod_map),
                  _const_spec(g_pre.shape), _const_spec(g_post.shape),
                  _const_spec(w1.shape), _const_spec(w2.shape)],
        out_specs=tok,
        compiler_params=_params(2),
        name="ffn",
    )(x, mod, g_pre, g_post, w1, w2)


def _rope_tables(n_tokens):
    t = jnp.arange(n_tokens)
    row = (t // GRID_W).astype(F32)
    col = (t % GRID_W).astype(F32)
    n_freq = HEAD_DIM // 4
    inv = 1.0 / (ROPE_THETA ** (jnp.arange(n_freq, dtype=F32) / n_freq))
    ang_r = row[:, None] * inv[None, :]
    ang_c = col[:, None] * inv[None, :]
    cos = jnp.concatenate([jnp.cos(ang_r)] * 2 + [jnp.cos(ang_c)] * 2, axis=-1)
    sin = jnp.concatenate([-jnp.sin(ang_r), jnp.sin(ang_r), -jnp.sin(ang_c), jnp.sin(ang_c)], axis=-1)
    return jnp.tile(cos, (1, 2)), jnp.tile(sin, (1, 2))


def kernel(x, c, ctx, c_ctx, w_ada, b_ada, g_pre1, g_post1, g_pre2, g_post2, w_in, q_norm, k_norm,
           w_attn_o, pool_w, pool_scale, w_pool_o, conv_dw, conv_b, conv_ln_g, conv_ln_b, w_conv_o,
           w_out, w_ffn_in, w_ffn_out):
    batch, n_lat, d = x.shape
    n_ctx = ctx.shape[1]
    depth = w_ada.shape[0]
    ctx_row = batch

    cc = jnp.zeros((MOD_ROWS, d), F32).at[:batch].set(c).at[ctx_row].set(c_ctx)
    mod_all = _ada(cc, w_ada, b_ada).reshape(depth, MOD_ROWS, 1, 6 * d)

    rope = _rope_tables(n_lat)
    lane_head = jnp.arange(ATTN_W) // HEAD_DIM
    headmat = jnp.where(lane_head[:, None] == lane_head[None, :], 1.0 / HEAD_DIM, 0.0).astype(BF16)

    lat_tile = 512
    ctx_tile = n_ctx
    e = ctx
    for i in range(depth):
        last = i == depth - 1
        mod = mod_all[i]
        row = lambda a: a[i].reshape(1, -1)
        w_in_b = w_in[i].astype(BF16)
        w_proj = w_in_b[:, :C_END]
        q_gain = jnp.tile(q_norm[i], N_Q_HEADS).reshape(1, -1)
        k_gain = jnp.tile(k_norm[i], N_KV_HEADS).reshape(1, -1)
        lw = {"g_pre1": row(g_pre1), "g_post1": row(g_post1), "w_gate": w_in_b[:, C_END:],
              "w_attn_o": w_attn_o[i].astype(BF16), "pool_w": pool_w[i].astype(BF16),
              "pool_scale": row(pool_scale), "w_pool_o": w_pool_o[i].astype(BF16),
              "conv_dw": jnp.repeat(conv_dw[i], V7X_SUBLANES, axis=0), "conv_b": row(conv_b), "conv_ln_g": row(conv_ln_g),
              "conv_ln_b": row(conv_ln_b), "w_conv_o": w_conv_o[i].astype(BF16),
              "w_out": w_out[i].astype(BF16)}
        w1 = w_ffn_in[i].astype(BF16)
        w2 = w_ffn_out[i].astype(BF16)

        if last:
            kc, vc = _in_proj(e, mod, ctx_row, lw["g_pre1"], w_proj[:, Q_END:V_END], headmat, q_gain, k_gain,
                              None, tile=ctx_tile, kv_only=True)
        else:
            qc, kc, vc, upc, gcc = _in_proj(e, mod, ctx_row, lw["g_pre1"], w_proj, headmat, q_gain, k_gain,
                                            None, tile=ctx_tile)
        qx, kx, vx, upx, gcx = _in_proj(x, mod, None, lw["g_pre1"], w_proj, headmat, q_gain, k_gain,
                                        rope, tile=lat_tile)
        ax = _attention(qx, (kc, kx), (vc, vx), rows=512)
        x = _mixer(x, ax, upx, gcx, mod, None, lw, tile=lat_tile)
        if not last:
            ac = _attention(qc, (kc,), (vc,), rows=ctx_tile)
            e = _mixer(e, ac, upc, gcc, mod, ctx_row, lw, tile=ctx_tile)

        x = _ffn(x, mod, None, row(g_pre2), row(g_post2), w1, w2, tile=lat_tile)
        if not last:
            e = _ffn(e, mod, ctx_row, row(g_pre2), row(g_post2), w1, w2, tile=ctx_tile)
    return x
```

```python
import functools

import jax
import jax.numpy as jnp
from jax import lax
from jax.experimental import pallas as pl
from jax.experimental.pallas import tpu as pltpu

D_MODEL = 1024
HEAD_DIM = 64
N_Q_HEADS = 8
N_KV_HEADS = 2
Q_GROUP = N_Q_HEADS // N_KV_HEADS
ATTN_W = N_Q_HEADS * HEAD_DIM
KV_W = N_KV_HEADS * HEAD_DIM
GROUP_W = Q_GROUP * HEAD_DIM
POOL_WINDOWS = (2, 4, 8, 16)
POOL_W = D_MODEL // 2
POOL_GROUP_W = POOL_W // len(POOL_WINDOWS)
CONV_W = D_MODEL // 2
CONV_K = 31
GRID_W = 64
ROPE_THETA = 10000.0
EPS = 1e-6
LOG2_E = 1.4426950408889634
Q_END = ATTN_W
K_END = Q_END + KV_W
V_END = K_END + KV_W
P_END = V_END + POOL_W
C_END = P_END + 2 * CONV_W
D_FF = -(-8 * D_MODEL // (3 * 256)) * 256
FF_CHUNKS = ((0, 1024), (1024, 2048), (2048, D_FF))

V7X_LANES = 128
V7X_SUBLANES = 8
V7X_MXU_DIM = 256
POOL_HALO = 8
POOL_SLAB_PAD = 128
CONV_HALO = 16
CONV_ROWS = 32
MOD_ROWS = 24
VMEM_LIMIT = 56 * 1024 * 1024

F32 = jnp.float32
BF16 = jnp.bfloat16


def _const_spec(shape):
    return pl.BlockSpec(shape, lambda *_: (0,) * len(shape), pipeline_mode=pl.Buffered(1))


def _params(n_axes):
    return pltpu.CompilerParams(dimension_semantics=("arbitrary",) * n_axes,
                                vmem_limit_bytes=VMEM_LIMIT)


def _dot(a, b):
    return jnp.dot(a, b, preferred_element_type=F32)


def _sigmoid(z):
    return 0.5 * jnp.tanh(0.5 * z) + 0.5


def _modulated_rmsnorm(x, g, shift, scale):
    y = x * lax.rsqrt(jnp.mean(x * x, axis=-1, keepdims=True) + EPS)
    return (y * g) * (1.0 + scale) + shift


def _ada_kernel(c_ref, w_ref, b_ref, o_ref):
    c = c_ref[...]
    s = c * jax.nn.sigmoid(c)
    o_ref[...] = jnp.dot(s, w_ref[...], preferred_element_type=F32,
                         precision=lax.Precision.HIGHEST) + b_ref[...]


def _ada(cc, w_ada, b_ada):
    depth, d, n = w_ada.shape
    bn = 1536
    return pl.pallas_call(
        _ada_kernel,
        out_shape=jax.ShapeDtypeStruct((depth, MOD_ROWS, n), F32),
        grid=(depth, n // bn),
        in_specs=[pl.BlockSpec((MOD_ROWS, d), lambda l, j: (0, 0)),
                  pl.BlockSpec((None, d, bn), lambda l, j: (l, 0, j)),
                  pl.BlockSpec((None, 1, bn), lambda l, j: (l, 0, j))],
        out_specs=pl.BlockSpec((None, MOD_ROWS, bn), lambda l, j: (l, 0, j)),
        compiler_params=_params(2),
        name="adaln",
    )(cc, w_ada, b_ada.reshape(depth, 1, n))


def _head_rms(u, headmat, gain):
    ms = _dot((u * u).astype(BF16), headmat)
    return u * lax.rsqrt(ms + EPS) * gain


def _rope(u, cos, sin_signed):
    lane = lax.broadcasted_iota(jnp.int32, u.shape, 1)
    first_half = (lane % 32) < 16
    partner = jnp.where(first_half,
                        pltpu.roll(u, V7X_LANES - 16, 1),
                        pltpu.roll(u, 16, 1))
    return u * cos + partner * sin_signed


def _replicate_heads(a):
    lane = lax.broadcasted_iota(jnp.int32, a.shape, 1)
    low = lane < HEAD_DIM
    r = pltpu.roll(a, HEAD_DIM, 1)
    return jnp.where(low, a, r), jnp.where(low, r, a)


def _in_proj_kernel(*refs, rope, kv_only):
    it = iter(refs)
    x_ref, mod_ref, gpre_ref, w_ref, hm_ref, qg_ref, kg_ref = (next(it) for _ in range(7))
    cos_ref = sin_ref = None
    if rope:
        cos_ref, sin_ref = next(it), next(it)
    if kv_only:
        k_out, v_out = next(it), next(it)
    else:
        q_out, k_out, v_out, up_out, gc_out = (next(it) for _ in range(5))

    d = D_MODEL
    h = _modulated_rmsnorm(x_ref[...], gpre_ref[...], mod_ref[:, 0:d], mod_ref[:, d:2 * d])
    hb = h.astype(BF16)

    if kv_only:
        ukv = _dot(hb, w_ref[...])
    else:
        ukv = _dot(hb, w_ref[:, Q_END:V_END])
    k = _head_rms(ukv[:, :KV_W], hm_ref[0:KV_W, 0:KV_W], kg_ref[...])
    if rope:
        k = _rope(k, cos_ref[...], sin_ref[...])
    h0, h1 = _replicate_heads(k)
    k_out[:, 0:GROUP_W] = jnp.concatenate([h0, h0], axis=1).astype(BF16)
    k_out[:, GROUP_W:2 * GROUP_W] = jnp.concatenate([h1, h1], axis=1).astype(BF16)
    v = ukv[:, KV_W:]
    low = lax.broadcasted_iota(jnp.int32, v.shape, 1) < HEAD_DIM
    vr = pltpu.roll(v, HEAD_DIM, 1)
    patterns = (jnp.where(low, v, 1.0), jnp.where(low, 1.0, vr),
                jnp.where(low, vr, 1.0), jnp.where(low, 1.0, v))
    for i, pat in enumerate(patterns):
        v_out[i] = jnp.concatenate([pat, pat], axis=1).astype(BF16)
    if kv_only:
        return

    q = _head_rms(_dot(hb, w_ref[:, 0:Q_END]), hm_ref[...], qg_ref[...])
    for c in range(ATTN_W // V7X_LANES):
        sl = slice(c * V7X_LANES, (c + 1) * V7X_LANES)
        qc = q[:, sl]
        if rope:
            qc = _rope(qc, cos_ref[...], sin_ref[...])
        q_out[:, sl] = (qc * (HEAD_DIM ** -0.5 * LOG2_E)).astype(BF16)

    up_out[...] = _dot(hb, w_ref[:, V_END:P_END])
    uc = _dot(hb, w_ref[:, P_END:C_END])
    gc_out[...] = uc[:, :CONV_W] * _sigmoid(uc[:, CONV_W:])


def _in_proj(x, mod, mod_row, g_pre, w, headmat, q_gain, k_gain, rope_tabs, *, tile, kv_only=False):
    bn, ln, d = x.shape
    rope = rope_tabs is not None
    if mod_row is None:
        mod_map = lambda b, j: (b, 0, 0)
    else:
        mod_map = lambda b, j: (mod_row, 0, 0)
    in_specs = [pl.BlockSpec((None, tile, d), lambda b, j: (b, j, 0)),
                pl.BlockSpec((None, 1, mod.shape[-1]), mod_map),
                _const_spec(g_pre.shape), _const_spec(w.shape), _const_spec(headmat.shape),
                _const_spec(q_gain.shape), _const_spec(k_gain.shape)]
    args = [x, mod, g_pre, w, headmat, q_gain, k_gain]
    if rope:
        in_specs += [pl.BlockSpec((tile, V7X_LANES), lambda b, j: (j, 0))] * 2
        args += list(rope_tabs)

    def tok_spec(width):
        return pl.BlockSpec((None, tile, width), lambda b, j: (b, j, 0))

    k_rep = jax.ShapeDtypeStruct((bn, ln, 2 * GROUP_W), BF16)
    v_rep = jax.ShapeDtypeStruct((bn, 2 * N_KV_HEADS, ln, GROUP_W), BF16)
    v_spec = pl.BlockSpec((None, 2 * N_KV_HEADS, tile, GROUP_W), lambda b, j: (b, 0, j, 0))
    if kv_only:
        out_shape = (k_rep, v_rep)
        out_specs = (tok_spec(2 * GROUP_W), v_spec)
    else:
        out_shape = (jax.ShapeDtypeStruct((bn, ln, ATTN_W), BF16), k_rep, v_rep,
                     jax.ShapeDtypeStruct((bn, ln, POOL_W), F32),
                     jax.ShapeDtypeStruct((bn, ln, CONV_W), F32))
        out_specs = (tok_spec(ATTN_W), tok_spec(2 * GROUP_W), v_spec,
                     tok_spec(POOL_W), tok_spec(CONV_W))
    return pl.pallas_call(
        functools.partial(_in_proj_kernel, rope=rope, kv_only=kv_only),
        out_shape=out_shape, grid=(bn, ln // tile), in_specs=in_specs, out_specs=out_specs,
        compiler_params=_params(2),
        name="in_proj_kv" if kv_only else "in_proj",
    )(*args)


def _attn_kernel(*refs, n_src, rows):
    q_ref = refs[0]
    k_refs = refs[1:1 + n_src]
    v_refs = refs[1 + n_src:1 + 2 * n_src]
    o_ref = refs[1 + 2 * n_src]
    acc_ref, s_a, s_b, m_a, m_b = refs[2 + 2 * n_src:]
    n_chunks = q_ref.shape[0] // rows
    head_of_lane = lax.broadcasted_iota(jnp.int32, (rows, GROUP_W), 1) // HEAD_DIM
    k_widths = [k.shape[0] for k in k_refs]

    def chunk_rows(r):
        if isinstance(r, int):
            return pl.ds(r * rows, rows)
        return pl.ds(pl.multiple_of(r * rows, rows), rows)

    def key_blocks():
        col = 0
        for src, width in enumerate(k_widths):
            for c0 in range(0, width, V7X_MXU_DIM):
                yield src, c0, col + c0
            col += width

    def scores(r, g, s_ref, m_ref):
        q = q_ref[chunk_rows(r), :]
        qg = jnp.where(head_of_lane == g, q, jnp.zeros_like(q))
        m = None
        for src, c0, col in key_blocks():
            s = lax.dot_general(qg, k_refs[src][c0:c0 + V7X_MXU_DIM, :], (((1,), (1,)), ((), ())),
                                preferred_element_type=F32)
            s_ref[:, col:col + V7X_MXU_DIM] = s
            blk = jnp.maximum(s[:, :V7X_LANES], s[:, V7X_LANES:])
            m = blk if m is None else jnp.maximum(m, blk)
        m_ref[...] = jnp.broadcast_to(jnp.max(m, axis=-1, keepdims=True), m_ref.shape)

    def softmax_pv(r, g, s_ref, m_ref):
        m = m_ref[...]
        lane_pattern = g % 2
        res = None
        for src, c0, col in key_blocks():
            p = [jnp.exp2(s_ref[:, col + i * V7X_LANES:col + (i + 1) * V7X_LANES] - m).astype(BF16)
                 for i in range(V7X_MXU_DIM // V7X_LANES)]
            part = _dot(jnp.concatenate(p, axis=1), v_refs[src][lane_pattern, c0:c0 + V7X_MXU_DIM, :])
            res = part if res is None else res + part
        norm = [res[:, h * V7X_LANES:(h + 1) * V7X_LANES] for h in range(GROUP_W // V7X_LANES)]
        norm = jnp.concatenate([n / pltpu.roll(n, HEAD_DIM, 1) for n in norm], axis=1)
        mine = head_of_lane == g
        if g == 0:
            acc_ref[...] = jnp.where(mine, norm, 0.0)
        elif g < Q_GROUP - 1:
            acc_ref[...] = jnp.where(mine, norm, acc_ref[...])
        else:
            o_ref[chunk_rows(r), :] = jnp.where(mine, norm, acc_ref[...]).astype(o_ref.dtype)

    def chunk(r, carry, prefetch_next=True):
        bufs = ((s_a, m_a), (s_b, m_b))
        for g in range(Q_GROUP):
            if g + 1 < Q_GROUP:
                scores(r, g + 1, *bufs[(g + 1) % 2])
            elif prefetch_next:
                scores(r + 1, 0, *bufs[0])
            softmax_pv(r, g, *bufs[g % 2])
        return carry

    scores(0, 0, s_a, m_a)
    lax.fori_loop(0, n_chunks - 1, chunk, 0)
    chunk(n_chunks - 1, 0, prefetch_next=False)


def _attention(q, k_srcs, v_srcs, *, rows):
    bn, ln, _ = q.shape
    n_src = len(k_srcs)
    n_keys = sum(k.shape[1] for k in k_srcs)
    q_spec = pl.BlockSpec((None, ln, GROUP_W), lambda b, h: (b, 0, h))
    kv_specs = ([pl.BlockSpec((None, k.shape[1], GROUP_W), lambda b, h: (b, 0, h)) for k in k_srcs]
                + [pl.BlockSpec((None, 2, v.shape[2], GROUP_W), lambda b, h: (b, h, 0, 0)) for v in v_srcs])
    return pl.pallas_call(
        functools.partial(_attn_kernel, n_src=n_src, rows=rows),
        out_shape=jax.ShapeDtypeStruct((bn, ln, ATTN_W), BF16),
        grid=(bn, N_KV_HEADS),
        in_specs=[q_spec] + kv_specs, out_specs=q_spec,
        scratch_shapes=[pltpu.VMEM((rows, GROUP_W), F32),
                        pltpu.VMEM((rows, n_keys), F32), pltpu.VMEM((rows, n_keys), F32),
                        pltpu.VMEM((rows, V7X_LANES), F32), pltpu.VMEM((rows, V7X_LANES), F32)],
        compiler_params=_params(2),
        name="attention",
    )(q, *k_srcs, *v_srcs)


def _mixer_kernel(x_ref, o_ref, up_ref, up_prev_ref, up_next_ref, gc_ref, gc_prev_ref, gc_next_ref,
                  icnt_ref, mod_ref, band_ref, gpre_ref, gpost_ref, wgate_ref, wao_ref, poolw_ref, pscale_ref,
                  wpo_ref, cdw_ref, cb_ref, lng_ref, lnb_ref, wco_ref, wout_ref,
                  out_ref, ppad_ref, gpad_ref, gsh_ref, cacc_ref):
    tile = x_ref.shape[0]
    d = D_MODEL
    j = pl.program_id(1)
    first = j == 0
    last = j == pl.num_programs(1) - 1

    x = x_ref[...]
    hb = _modulated_rmsnorm(x, gpre_ref[...], mod_ref[:, 0:d], mod_ref[:, d:2 * d]).astype(BF16)

    merged = _sigmoid(_dot(hb, wgate_ref[:, 0:d])) * _dot(o_ref[...], wao_ref[...])

    ppad_ref[0:POOL_HALO, :] = jnp.where(first, 0.0, up_prev_ref[...])
    ppad_ref[POOL_HALO:POOL_HALO + tile, :] = up_ref[...]
    ppad_ref[POOL_HALO + tile:2 * POOL_HALO + tile, :] = jnp.where(last, 0.0, up_next_ref[...])
    ppad_ref[2 * POOL_HALO + tile:, :] = jnp.zeros((ppad_ref.shape[0] - 2 * POOL_HALO - tile, POOL_W), F32)
    pb = ppad_ref[...].astype(BF16)
    pooled = []
    for gi in range(len(POOL_WINDOWS)):
        cols = slice(gi * POOL_GROUP_W, (gi + 1) * POOL_GROUP_W)
        dg = _dot(band_ref[gi], pb[:, cols]) * icnt_ref[:, cols] - up_ref[:, cols]
        pooled.append(_dot(dg.astype(BF16), poolw_ref[gi]))
    yb = jnp.concatenate(pooled, axis=-1) * pscale_ref[...]
    merged = merged + _sigmoid(_dot(hb, wgate_ref[:, d:2 * d])) * _dot(yb.astype(BF16), wpo_ref[...])

    gpad_ref[0:CONV_HALO, :] = jnp.where(first, 0.0, gc_prev_ref[...])
    gpad_ref[CONV_HALO:CONV_HALO + tile, :] = gc_ref[...]
    gpad_ref[CONV_HALO + tile:, :] = jnp.where(last, 0.0, gc_next_ref[...])
    tap0 = CONV_HALO - CONV_K // 2
    shifted_rows = gsh_ref.shape[1]
    for r in range(V7X_SUBLANES):
        gsh_ref[r] = gpad_ref[r:r + shifted_rows, :]
    groups = CONV_ROWS // V7X_SUBLANES
    for r0 in range(0, tile, CONV_ROWS):
        acc = jnp.broadcast_to(cb_ref[...], (groups, V7X_SUBLANES, CONV_W))
        for k in range(CONV_K):
            phase = (tap0 + k) % V7X_SUBLANES
            base = (tap0 + k) - phase
            g = gsh_ref[phase, r0 + base:r0 + base + CONV_ROWS, :].reshape(groups, V7X_SUBLANES, CONV_W)
            acc = acc + cdw_ref[k * V7X_SUBLANES:(k + 1) * V7X_SUBLANES, :][None] * g
        cacc_ref[r0:r0 + CONV_ROWS, :] = acc.reshape(CONV_ROWS, CONV_W)
    yc = cacc_ref[...]
    cen = yc - jnp.mean(yc, axis=-1, keepdims=True)
    var = jnp.mean(cen * cen, axis=-1, keepdims=True)
    yc = cen * lax.rsqrt(var + EPS) * lng_ref[...] + lnb_ref[...]
    yc = (yc * _sigmoid(yc)).astype(BF16)
    merged = merged + _sigmoid(_dot(hb, wgate_ref[:, 2 * d:3 * d])) * _dot(yc, wco_ref[...])

    y = _dot(merged.astype(BF16), wout_ref[...])
    yn = y * lax.rsqrt(jnp.mean(y * y, axis=-1, keepdims=True) + EPS) * gpost_ref[...]
    out_ref[...] = x + mod_ref[:, 2 * d:3 * d] * yn


def _pool_tables(n_tokens, tile):
    slab = tile + POOL_SLAB_PAD
    t = jnp.arange(tile)[:, None]
    i = jnp.arange(slab)[None, :] - POOL_HALO
    band = jnp.stack([(i >= t - w // 2) & (i < t + w // 2) for w in POOL_WINDOWS]).astype(BF16)
    pos = jnp.arange(n_tokens)
    inv = [1.0 / (jnp.minimum(pos + w // 2, n_tokens) - jnp.maximum(pos - w // 2, 0)).astype(F32)
           for w in POOL_WINDOWS]
    return band, jnp.repeat(jnp.stack(inv, axis=-1), POOL_GROUP_W, axis=-1)


def _mixer(x, attn, up, gc, mod, mod_row, lw, *, tile):
    bn, ln, d = x.shape
    n_tiles = ln // tile
    if mod_row is None:
        mod_map = lambda b, j: (b, 0, 0)
    else:
        mod_map = lambda b, j: (mod_row, 0, 0)

    def tok_spec(width):
        return pl.BlockSpec((None, tile, width), lambda b, j: (b, j, 0))

    def halo_specs(width, halo):
        per_tile = tile // halo
        n_blocks = ln // halo
        prev = pl.BlockSpec((None, halo, width), lambda b, j: (b, jnp.maximum(j * per_tile - 1, 0), 0))
        nxt = pl.BlockSpec((None, halo, width),
                           lambda b, j: (b, jnp.minimum((j + 1) * per_tile, n_blocks - 1), 0))
        return [tok_spec(width), prev, nxt]

    band, inv_cnt = _pool_tables(ln, tile)
    weights = [band, lw["g_pre1"], lw["g_post1"], lw["w_gate"], lw["w_attn_o"], lw["pool_w"], lw["pool_scale"],
               lw["w_pool_o"], lw["conv_dw"], lw["conv_b"], lw["conv_ln_g"], lw["conv_ln_b"],
               lw["w_conv_o"], lw["w_out"]]
    in_specs = ([tok_spec(d), tok_spec(ATTN_W)] + halo_specs(POOL_W, POOL_HALO) + halo_specs(CONV_W, CONV_HALO)
                + [pl.BlockSpec((tile, POOL_W), lambda b, j: (j, 0)),
                   pl.BlockSpec((None, 1, mod.shape[-1]), mod_map)] + [_const_spec(w.shape) for w in weights])
    return pl.pallas_call(
        _mixer_kernel,
        out_shape=jax.ShapeDtypeStruct(x.shape, F32),
        grid=(bn, n_tiles), in_specs=in_specs, out_specs=tok_spec(d),
        scratch_shapes=[pltpu.VMEM((band.shape[-1], POOL_W), F32),
                        pltpu.VMEM((tile + 2 * CONV_HALO, CONV_W), F32),
                        pltpu.VMEM((V7X_SUBLANES, tile + 2 * CONV_HALO - V7X_SUBLANES, CONV_W), F32),
                        pltpu.VMEM((tile, CONV_W), F32)],
        compiler_params=_params(2),
        name="mixer",
    )(x, attn, up, up, up, gc, gc, gc, inv_cnt, mod, *weights)


def _ffn_kernel(x_ref, mod_ref, gpre_ref, gpost_ref, w1_ref, w2_ref, out_ref):
    d = D_MODEL
    x = x_ref[...]
    hb = _modulated_rmsnorm(x, gpre_ref[...], mod_ref[:, 3 * d:4 * d], mod_ref[:, 4 * d:5 * d]).astype(BF16)
    y = None
    for lo, hi in FF_CHUNKS:
        a = _dot(hb, w1_ref[:, lo:hi])
        b = _dot(hb, w1_ref[:, D_FF + lo:D_FF + hi])
        part = _dot((a * _sigmoid(a) * b).astype(BF16), w2_ref[lo:hi, :])
        y = part if y is None else y + part
    yn = y * lax.rsqrt(jnp.mean(y * y, axis=-1, keepdims=True) + EPS) * gpost_ref[...]
    out_ref[...] = x + mod_ref[:, 5 * d:6 * d] * yn


def _ffn(x, mod, mod_row, g_pre, g_post, w1, w2, *, tile):
    bn, ln, d = x.shape
    if mod_row is None:
        mod_map = lambda b, j: (b, 0, 0)
    else:
        mod_map = lambda b, j: (mod_row, 0, 0)
    tok = pl.BlockSpec((None, tile, d), lambda b, j: (b, j, 0))
    return pl.pallas_call(
        _ffn_kernel,
        out_shape=jax.ShapeDtypeStruct(x.shape, F32),
        grid=(bn, ln // tile),
        in_specs=[tok, pl.BlockSpec((None, 1, mod.shape[-1]), mod_map),
                  _const_spec(g_pre.shape), _const_spec(g_post.shape),
                  _const_spec(w1.shape), _const_spec(w2.shape)],
        out_specs=tok,
        compiler_params=_params(2),
        name="ffn",
    )(x, mod, g_pre, g_post, w1, w2)


def _rope_tables(n_tokens):
    t = jnp.arange(n_tokens)
    row = (t // GRID_W).astype(F32)
    col = (t % GRID_W).astype(F32)
    n_freq = HEAD_DIM // 4
    inv = 1.0 / (ROPE_THETA ** (jnp.arange(n_freq, dtype=F32) / n_freq))
    ang_r = row[:, None] * inv[None, :]
    ang_c = col[:, None] * inv[None, :]
    cos = jnp.concatenate([jnp.cos(ang_r)] * 2 + [jnp.cos(ang_c)] * 2, axis=-1)
    sin = jnp.concatenate([-jnp.sin(ang_r), jnp.sin(ang_r), -jnp.sin(ang_c), jnp.sin(ang_c)], axis=-1)
    return jnp.tile(cos, (1, 2)), jnp.tile(sin, (1, 2))


def kernel(x, c, ctx, c_ctx, w_ada, b_ada, g_pre1, g_post1, g_pre2, g_post2, w_in, q_norm, k_norm,
           w_attn_o, pool_w, pool_scale, w_pool_o, conv_dw, conv_b, conv_ln_g, conv_ln_b, w_conv_o,
           w_out, w_ffn_in, w_ffn_out):
    batch, n_lat, d = x.shape
    n_ctx = ctx.shape[1]
    depth = w_ada.shape[0]
    ctx_row = batch

    cc = jnp.zeros((MOD_ROWS, d), F32).at[:batch].set(c).at[ctx_row].set(c_ctx)
    mod_all = _ada(cc, w_ada, b_ada).reshape(depth, MOD_ROWS, 1, 6 * d)

    rope = _rope_tables(n_lat)
    lane_head = jnp.arange(ATTN_W) // HEAD_DIM
    headmat = jnp.where(lane_head[:, None] == lane_head[None, :], 1.0 / HEAD_DIM, 0.0).astype(BF16)

    lat_tile = 512
    ctx_tile = n_ctx
    e = ctx
    for i in range(depth):
        last = i == depth - 1
        mod = mod_all[i]
        row = lambda a: a[i].reshape(1, -1)
        w_in_b = w_in[i].astype(BF16)
        w_proj = w_in_b[:, :C_END]
        q_gain = jnp.tile(q_norm[i], N_Q_HEADS).reshape(1, -1)
        k_gain = jnp.tile(k_norm[i], N_KV_HEADS).reshape(1, -1)
        lw = {"g_pre1": row(g_pre1), "g_post1": row(g_post1), "w_gate": w_in_b[:, C_END:],
              "w_attn_o": w_attn_o[i].astype(BF16), "pool_w": pool_w[i].astype(BF16),
              "pool_scale": row(pool_scale), "w_pool_o": w_pool_o[i].astype(BF16),
              "conv_dw": jnp.repeat(conv_dw[i], V7X_SUBLANES, axis=0), "conv_b": row(conv_b), "conv_ln_g": row(conv_ln_g),
              "conv_ln_b": row(conv_ln_b), "w_conv_o": w_conv_o[i].astype(BF16),
              "w_out": w_out[i].astype(BF16)}
        w1 = w_ffn_in[i].astype(BF16)
        w2 = w_ffn_out[i].astype(BF16)

        if last:
            kc, vc = _in_proj(e, mod, ctx_row, lw["g_pre1"], w_proj[:, Q_END:V_END], headmat, q_gain, k_gain,
                              None, tile=ctx_tile, kv_only=True)
        else:
            qc, kc, vc, upc, gcc = _in_proj(e, mod, ctx_row, lw["g_pre1"], w_proj, headmat, q_gain, k_gain,
                                            None, tile=ctx_tile)
        qx, kx, vx, upx, gcx = _in_proj(x, mod, None, lw["g_pre1"], w_proj, headmat, q_gain, k_gain,
                                        rope, tile=lat_tile)
        ax = _attention(qx, (kc, kx), (vc, vx), rows=512)
        x = _mixer(x, ax, upx, gcx, mod, None, lw, tile=lat_tile)
        if not last:
            ac = _attention(qc, (kc,), (vc,), rows=ctx_tile)
            e = _mixer(e, ac, upc, gcc, mod, ctx_row, lw, tile=ctx_tile)

        x = _ffn(x, mod, None, row(g_pre2), row(g_post2), w1, w2, tile=lat_tile)
        if not last:
            e = _ffn(e, mod, ctx_row, row(g_pre2), row(g_post2), w1, w2, tile=ctx_tile)
    return x
```

```python
import functools

import jax
import jax.numpy as jnp
from jax import lax
from jax.experimental import pallas as pl
from jax.experimental.pallas import tpu as pltpu

D_MODEL = 1024
HEAD_DIM = 64
N_Q_HEADS = 8
N_KV_HEADS = 2
Q_GROUP = N_Q_HEADS // N_KV_HEADS
ATTN_W = N_Q_HEADS * HEAD_DIM
KV_W = N_KV_HEADS * HEAD_DIM
GROUP_W = Q_GROUP * HEAD_DIM
POOL_WINDOWS = (2, 4, 8, 16)
POOL_W = D_MODEL // 2
POOL_GROUP_W = POOL_W // len(POOL_WINDOWS)
CONV_W = D_MODEL // 2
CONV_K = 31
GRID_W = 64
ROPE_THETA = 10000.0
EPS = 1e-6
LOG2_E = 1.4426950408889634
Q_END = ATTN_W
K_END = Q_END + KV_W
V_END = K_END + KV_W
P_END = V_END + POOL_W
C_END = P_END + 2 * CONV_W
D_FF = -(-8 * D_MODEL // (3 * 256)) * 256
FF_CHUNKS = ((0, 1024), (1024, 2048), (2048, D_FF))

V7X_LANES = 128
V7X_SUBLANES = 8
V7X_MXU_DIM = 256
POOL_HALO = 8
POOL_SLAB_PAD = 128
CONV_HALO = 16
CONV_ROWS = 32
CONV_LANES = 256
MOD_ROWS = 24
VMEM_LIMIT = 56 * 1024 * 1024

F32 = jnp.float32
BF16 = jnp.bfloat16


def _const_spec(shape):
    return pl.BlockSpec(shape, lambda *_: (0,) * len(shape), pipeline_mode=pl.Buffered(1))


def _params(n_axes, flags=None):
    return pltpu.CompilerParams(dimension_semantics=("arbitrary",) * n_axes,
                                vmem_limit_bytes=VMEM_LIMIT, flags=flags)


def _dot(a, b):
    return jnp.dot(a, b, preferred_element_type=F32)


def _sigmoid(z):
    return 0.5 * jnp.tanh(0.5 * z) + 0.5


def _modulated_rmsnorm(x, g, shift, scale):
    y = x * lax.rsqrt(jnp.mean(x * x, axis=-1, keepdims=True) + EPS)
    return (y * g) * (1.0 + scale) + shift


def _ada_kernel(c_ref, w_ref, b_ref, o_ref):
    c = c_ref[...]
    s = c * jax.nn.sigmoid(c)
    o_ref[...] = jnp.dot(s, w_ref[...], preferred_element_type=F32,
                         precision=lax.Precision.HIGHEST) + b_ref[...]


def _ada(cc, w_ada, b_ada):
    depth, d, n = w_ada.shape
    bn = 1536
    return pl.pallas_call(
        _ada_kernel,
        out_shape=jax.ShapeDtypeStruct((depth, MOD_ROWS, n), F32),
        grid=(depth, n // bn),
        in_specs=[pl.BlockSpec((MOD_ROWS, d), lambda l, j: (0, 0)),
                  pl.BlockSpec((None, d, bn), lambda l, j: (l, 0, j)),
                  pl.BlockSpec((None, 1, bn), lambda l, j: (l, 0, j))],
        out_specs=pl.BlockSpec((None, MOD_ROWS, bn), lambda l, j: (l, 0, j)),
        compiler_params=_params(2),
        name="adaln",
    )(cc, w_ada, b_ada.reshape(depth, 1, n))


def _head_rms(u, headmat, gain):
    ms = _dot((u * u).astype(BF16), headmat)
    return u * lax.rsqrt(ms + EPS) * gain


def _rope(u, cos, sin_signed):
    lane = lax.broadcasted_iota(jnp.int32, u.shape, 1)
    first_half = (lane % 32) < 16
    partner = jnp.where(first_half,
                        pltpu.roll(u, V7X_LANES - 16, 1),
                        pltpu.roll(u, 16, 1))
    return u * cos + partner * sin_signed


def _replicate_heads(a):
    lane = lax.broadcasted_iota(jnp.int32, a.shape, 1)
    low = lane < HEAD_DIM
    r = pltpu.roll(a, HEAD_DIM, 1)
    return jnp.where(low, a, r), jnp.where(low, r, a)


def _in_proj_kernel(*refs, rope, kv_only):
    it = iter(refs)
    x_ref, mod_ref, gpre_ref, w_ref, hm_ref, qg_ref, kg_ref = (next(it) for _ in range(7))
    cos_ref = sin_ref = None
    if rope:
        cos_ref, sin_ref = next(it), next(it)
    if kv_only:
        k_out, v_out = next(it), next(it)
    else:
        q_out, k_out, v_out, up_out, gc_out = (next(it) for _ in range(5))

    d = D_MODEL
    h = _modulated_rmsnorm(x_ref[...], gpre_ref[...], mod_ref[:, 0:d], mod_ref[:, d:2 * d])
    hb = h.astype(BF16)

    if kv_only:
        ukv = _dot(hb, w_ref[...])
    else:
        ukv = _dot(hb, w_ref[:, Q_END:V_END])
    k = _head_rms(ukv[:, :KV_W], hm_ref[0:KV_W, 0:KV_W], kg_ref[...])
    if rope:
        k = _rope(k, cos_ref[...], sin_ref[...])
    h0, h1 = _replicate_heads(k)
    k_out[:, 0:GROUP_W] = jnp.concatenate([h0, h0], axis=1).astype(BF16)
    k_out[:, GROUP_W:2 * GROUP_W] = jnp.concatenate([h1, h1], axis=1).astype(BF16)
    v = ukv[:, KV_W:]
    low = lax.broadcasted_iota(jnp.int32, v.shape, 1) < HEAD_DIM
    vr = pltpu.roll(v, HEAD_DIM, 1)
    patterns = (jnp.where(low, v, 1.0), jnp.where(low, 1.0, vr),
                jnp.where(low, vr, 1.0), jnp.where(low, 1.0, v))
    for i, pat in enumerate(patterns):
        v_out[i] = jnp.concatenate([pat, pat], axis=1).astype(BF16)
    if kv_only:
        return

    q = _head_rms(_dot(hb, w_ref[:, 0:Q_END]), hm_ref[...], qg_ref[...])
    for c in range(ATTN_W // V7X_LANES):
        sl = slice(c * V7X_LANES, (c + 1) * V7X_LANES)
        qc = q[:, sl]
        if rope:
            qc = _rope(qc, cos_ref[...], sin_ref[...])
        q_out[:, sl] = (qc * (HEAD_DIM ** -0.5 * LOG2_E)).astype(BF16)

    up_out[...] = _dot(hb, w_ref[:, V_END:P_END])
    uc = _dot(hb, w_ref[:, P_END:C_END])
    gc_out[...] = uc[:, :CONV_W] * _sigmoid(uc[:, CONV_W:])


def _in_proj(x, mod, mod_row, g_pre, w, headmat, q_gain, k_gain, rope_tabs, *, tile, kv_only=False):
    bn, ln, d = x.shape
    rope = rope_tabs is not None
    if mod_row is None:
        mod_map = lambda b, j: (b, 0, 0)
    else:
        mod_map = lambda b, j: (mod_row, 0, 0)
    in_specs = [pl.BlockSpec((None, tile, d), lambda b, j: (b, j, 0)),
                pl.BlockSpec((None, 1, mod.shape[-1]), mod_map),
                _const_spec(g_pre.shape), _const_spec(w.shape), _const_spec(headmat.shape),
                _const_spec(q_gain.shape), _const_spec(k_gain.shape)]
    args = [x, mod, g_pre, w, headmat, q_gain, k_gain]
    if rope:
        in_specs += [pl.BlockSpec((tile, V7X_LANES), lambda b, j: (j, 0))] * 2
        args += list(rope_tabs)

    def tok_spec(width):
        return pl.BlockSpec((None, tile, width), lambda b, j: (b, j, 0))

    k_rep = jax.ShapeDtypeStruct((bn, ln, 2 * GROUP_W), BF16)
    v_rep = jax.ShapeDtypeStruct((bn, 2 * N_KV_HEADS, ln, GROUP_W), BF16)
    v_spec = pl.BlockSpec((None, 2 * N_KV_HEADS, tile, GROUP_W), lambda b, j: (b, 0, j, 0))
    if kv_only:
        out_shape = (k_rep, v_rep)
        out_specs = (tok_spec(2 * GROUP_W), v_spec)
    else:
        out_shape = (jax.ShapeDtypeStruct((bn, ln, ATTN_W), BF16), k_rep, v_rep,
                     jax.ShapeDtypeStruct((bn, ln, POOL_W), F32),
                     jax.ShapeDtypeStruct((bn, ln, CONV_W), F32))
        out_specs = (tok_spec(ATTN_W), tok_spec(2 * GROUP_W), v_spec,
                     tok_spec(POOL_W), tok_spec(CONV_W))
    return pl.pallas_call(
        functools.partial(_in_proj_kernel, rope=rope, kv_only=kv_only),
        out_shape=out_shape, grid=(bn, ln // tile), in_specs=in_specs, out_specs=out_specs,
        compiler_params=_params(2),
        name="in_proj_kv" if kv_only else "in_proj",
    )(*args)


def _attn_kernel(*refs, n_src, rows):
    q_ref = refs[0]
    k_refs = refs[1:1 + n_src]
    v_refs = refs[1 + n_src:1 + 2 * n_src]
    o_ref = refs[1 + 2 * n_src]
    acc_ref, s_a, s_b, m_a, m_b = refs[2 + 2 * n_src:]
    n_chunks = q_ref.shape[0] // rows
    head_of_lane = lax.broadcasted_iota(jnp.int32, (rows, GROUP_W), 1) // HEAD_DIM
    k_widths = [k.shape[0] for k in k_refs]

    def chunk_rows(r):
        if isinstance(r, int):
            return pl.ds(r * rows, rows)
        return pl.ds(pl.multiple_of(r * rows, rows), rows)

    def key_blocks():
        col = 0
        for src, width in enumerate(k_widths):
            for c0 in range(0, width, V7X_MXU_DIM):
                yield src, c0, col + c0
            col += width

    def scores(r, g, s_ref, m_ref):
        q = q_ref[chunk_rows(r), :]
        qg = jnp.where(head_of_lane == g, q, jnp.zeros_like(q))
        m = None
        for src, c0, col in key_blocks():
            s = lax.dot_general(qg, k_refs[src][c0:c0 + V7X_MXU_DIM, :], (((1,), (1,)), ((), ())),
                                preferred_element_type=F32)
            s_ref[:, col:col + V7X_MXU_DIM] = s
            blk = jnp.maximum(s[:, :V7X_LANES], s[:, V7X_LANES:])
            m = blk if m is None else jnp.maximum(m, blk)
        m_ref[...] = jnp.broadcast_to(jnp.max(m, axis=-1, keepdims=True), m_ref.shape)

    def softmax_pv(r, g, s_ref, m_ref):
        m = m_ref[...]
        lane_pattern = g % 2
        res = None
        for src, c0, col in key_blocks():
            p = [jnp.exp2(s_ref[:, col + i * V7X_LANES:col + (i + 1) * V7X_LANES] - m).astype(BF16)
                 for i in range(V7X_MXU_DIM // V7X_LANES)]
            part = _dot(jnp.concatenate(p, axis=1), v_refs[src][lane_pattern, c0:c0 + V7X_MXU_DIM, :])
            res = part if res is None else res + part
        norm = [res[:, h * V7X_LANES:(h + 1) * V7X_LANES] for h in range(GROUP_W // V7X_LANES)]
        norm = jnp.concatenate([n / pltpu.roll(n, HEAD_DIM, 1) for n in norm], axis=1)
        mine = head_of_lane == g
        if g == 0:
            acc_ref[...] = jnp.where(mine, norm, 0.0)
        elif g < Q_GROUP - 1:
            acc_ref[...] = jnp.where(mine, norm, acc_ref[...])
        else:
            o_ref[chunk_rows(r), :] = jnp.where(mine, norm, acc_ref[...]).astype(o_ref.dtype)

    def chunk(r, carry, prefetch_next=True):
        bufs = ((s_a, m_a), (s_b, m_b))
        for g in range(Q_GROUP):
            if g + 1 < Q_GROUP:
                scores(r, g + 1, *bufs[(g + 1) % 2])
            elif prefetch_next:
                scores(r + 1, 0, *bufs[0])
            softmax_pv(r, g, *bufs[g % 2])
        return carry

    scores(0, 0, s_a, m_a)
    lax.fori_loop(0, n_chunks - 1, chunk, 0)
    chunk(n_chunks - 1, 0, prefetch_next=False)


def _attention(q, k_srcs, v_srcs, *, rows):
    bn, ln, _ = q.shape
    n_src = len(k_srcs)
    n_keys = sum(k.shape[1] for k in k_srcs)
    q_spec = pl.BlockSpec((None, ln, GROUP_W), lambda b, h: (b, 0, h))
    kv_specs = ([pl.BlockSpec((None, k.shape[1], GROUP_W), lambda b, h: (b, 0, h)) for k in k_srcs]
                + [pl.BlockSpec((None, 2, v.shape[2], GROUP_W), lambda b, h: (b, h, 0, 0)) for v in v_srcs])
    return pl.pallas_call(
        functools.partial(_attn_kernel, n_src=n_src, rows=rows),
        out_shape=jax.ShapeDtypeStruct((bn, ln, ATTN_W), BF16),
        grid=(bn, N_KV_HEADS),
        in_specs=[q_spec] + kv_specs, out_specs=q_spec,
        scratch_shapes=[pltpu.VMEM((rows, GROUP_W), F32),
                        pltpu.VMEM((rows, n_keys), F32), pltpu.VMEM((rows, n_keys), F32),
                        pltpu.VMEM((rows, V7X_LANES), F32), pltpu.VMEM((rows, V7X_LANES), F32)],
        compiler_params=_params(2),
        name="attention",
    )(q, *k_srcs, *v_srcs)


def _mixer_kernel(x_ref, o_ref, up_ref, up_prev_ref, up_next_ref, gc_ref, gc_prev_ref, gc_next_ref,
                  icnt_ref, mod_ref, band_ref, gpre_ref, gpost_ref, wgate_ref, wao_ref, poolw_ref, pscale_ref,
                  wpo_ref, cdw_ref, cb_ref, lng_ref, lnb_ref, wco_ref, wout_ref,
                  out_ref, ppad_ref, gpad_ref, gsh_ref, cacc_ref, hb_ref, yb_ref, merged_ref):
    tile = x_ref.shape[0]
    d = D_MODEL
    j = pl.program_id(1)
    first = j == 0
    last = j == pl.num_programs(1) - 1

    gpad_ref[0:CONV_HALO, :] = jnp.where(first, 0.0, gc_prev_ref[...])
    gpad_ref[CONV_HALO:CONV_HALO + tile, :] = gc_ref[...]
    gpad_ref[CONV_HALO + tile:, :] = jnp.where(last, 0.0, gc_next_ref[...])
    tap0 = CONV_HALO - CONV_K // 2
    shifted_rows = gsh_ref.shape[1]
    for r in range(V7X_SUBLANES):
        gsh_ref[r] = gpad_ref[r:r + shifted_rows, :]

    x = x_ref[...]
    hb_ref[...] = _modulated_rmsnorm(x, gpre_ref[...], mod_ref[:, 0:d], mod_ref[:, d:2 * d]).astype(BF16)

    ppad_ref[0:POOL_HALO, :] = jnp.where(first, 0.0, up_prev_ref[...])
    ppad_ref[POOL_HALO:POOL_HALO + tile, :] = up_ref[...]
    ppad_ref[POOL_HALO + tile:2 * POOL_HALO + tile, :] = jnp.where(last, 0.0, up_next_ref[...])
    ppad_ref[2 * POOL_HALO + tile:, :] = jnp.zeros((ppad_ref.shape[0] - 2 * POOL_HALO - tile, POOL_W), F32)
    pb = ppad_ref[...].astype(BF16)
    for gi in range(len(POOL_WINDOWS)):
        cols = slice(gi * POOL_GROUP_W, (gi + 1) * POOL_GROUP_W)
        dg = _dot(band_ref[gi], pb[:, cols]) * icnt_ref[:, cols] - up_ref[:, cols]
        yb_ref[:, cols] = (_dot(dg.astype(BF16), poolw_ref[gi]) * pscale_ref[:, cols]).astype(BF16)

    n_col = d // V7X_MXU_DIM
    rows_per_trip = tile // n_col
    groups = CONV_ROWS // V7X_SUBLANES

    def trip(n, carry):
        row0 = n * rows_per_trip if isinstance(n, int) else pl.multiple_of(n * rows_per_trip, rows_per_trip)
        for r0, c0 in ((r, c) for r in range(0, rows_per_trip, CONV_ROWS)
                       for c in range(0, CONV_W, CONV_LANES)):
            lanes = slice(c0, c0 + CONV_LANES)
            acc = jnp.broadcast_to(cb_ref[:, lanes], (groups, V7X_SUBLANES, CONV_LANES))
            for phase in range(V7X_SUBLANES):
                bases = [tap0 + k - phase for k in range(CONV_K) if (tap0 + k) % V7X_SUBLANES == phase]
                span = bases[-1] - bases[0] + CONV_ROWS
                slab = gsh_ref[phase, pl.ds(row0 + (r0 + bases[0]), span), lanes]
                slab = slab.reshape(span // V7X_SUBLANES, V7X_SUBLANES, CONV_LANES)
                for base in bases:
                    k = base + phase - tap0
                    g0 = (base - bases[0]) // V7X_SUBLANES
                    acc = acc + (cdw_ref[k * V7X_SUBLANES:(k + 1) * V7X_SUBLANES, lanes][None]
                                 * slab[g0:g0 + groups])
            cacc_ref[pl.ds(row0 + r0, CONV_ROWS), lanes] = acc.reshape(CONV_ROWS, CONV_LANES)
        hb = hb_ref[...]
        merged_ref[n] = (_sigmoid(_dot(hb, wgate_ref[n])) * _dot(o_ref[...], wao_ref[n])
                         + _sigmoid(_dot(hb, wgate_ref[n_col + n])) * _dot(yb_ref[...], wpo_ref[n]))
        return carry

    lax.fori_loop(0, n_col, trip, 0)

    yc = cacc_ref[...]
    cen = yc - jnp.mean(yc, axis=-1, keepdims=True)
    var = jnp.mean(cen * cen, axis=-1, keepdims=True)
    yc = cen * lax.rsqrt(var + EPS) * lng_ref[...] + lnb_ref[...]
    yc = (yc * _sigmoid(yc)).astype(BF16)
    hb = hb_ref[...]
    y = None
    for n in range(n_col):
        cols = slice(n * V7X_MXU_DIM, (n + 1) * V7X_MXU_DIM)
        merged = merged_ref[n] + _sigmoid(_dot(hb, wgate_ref[2 * n_col + n])) * _dot(yc, wco_ref[:, cols])
        part = _dot(merged.astype(BF16), wout_ref[cols, :])
        y = part if y is None else y + part

    yn = y * lax.rsqrt(jnp.mean(y * y, axis=-1, keepdims=True) + EPS) * gpost_ref[...]
    out_ref[...] = x + mod_ref[:, 2 * d:3 * d] * yn


def _pool_tables(n_tokens, tile):
    slab = tile + POOL_SLAB_PAD
    t = jnp.arange(tile)[:, None]
    i = jnp.arange(slab)[None, :] - POOL_HALO
    band = jnp.stack([(i >= t - w // 2) & (i < t + w // 2) for w in POOL_WINDOWS]).astype(BF16)
    pos = jnp.arange(n_tokens)
    inv = [1.0 / (jnp.minimum(pos + w // 2, n_tokens) - jnp.maximum(pos - w // 2, 0)).astype(F32)
           for w in POOL_WINDOWS]
    return band, jnp.repeat(jnp.stack(inv, axis=-1), POOL_GROUP_W, axis=-1)


def _mixer(x, attn, up, gc, mod, mod_row, lw, *, tile):
    bn, ln, d = x.shape
    n_tiles = ln // tile
    if mod_row is None:
        mod_map = lambda b, j: (b, 0, 0)
    else:
        mod_map = lambda b, j: (mod_row, 0, 0)

    def tok_spec(width):
        return pl.BlockSpec((None, tile, width), lambda b, j: (b, j, 0))

    def halo_specs(width, halo):
        per_tile = tile // halo
        n_blocks = ln // halo
        prev = pl.BlockSpec((None, halo, width), lambda b, j: (b, jnp.maximum(j * per_tile - 1, 0), 0))
        nxt = pl.BlockSpec((None, halo, width),
                           lambda b, j: (b, jnp.minimum((j + 1) * per_tile, n_blocks - 1), 0))
        return [tok_spec(width), prev, nxt]

    band, inv_cnt = _pool_tables(ln, tile)
    weights = [band, lw["g_pre1"], lw["g_post1"], lw["w_gate"], lw["w_attn_o"], lw["pool_w"], lw["pool_scale"],
               lw["w_pool_o"], lw["conv_dw"], lw["conv_b"], lw["conv_ln_g"], lw["conv_ln_b"],
               lw["w_conv_o"], lw["w_out"]]
    in_specs = ([tok_spec(d), tok_spec(ATTN_W)] + halo_specs(POOL_W, POOL_HALO) + halo_specs(CONV_W, CONV_HALO)
                + [pl.BlockSpec((tile, POOL_W), lambda b, j: (j, 0)),
                   pl.BlockSpec((None, 1, mod.shape[-1]), mod_map)] + [_const_spec(w.shape) for w in weights])
    return pl.pallas_call(
        _mixer_kernel,
        out_shape=jax.ShapeDtypeStruct(x.shape, F32),
        grid=(bn, n_tiles), in_specs=in_specs, out_specs=tok_spec(d),
        scratch_shapes=[pltpu.VMEM((band.shape[-1], POOL_W), F32),
                        pltpu.VMEM((tile + 2 * CONV_HALO, CONV_W), F32),
                        pltpu.VMEM((V7X_SUBLANES, tile + 2 * CONV_HALO - V7X_SUBLANES, CONV_W), F32),
                        pltpu.VMEM((tile, CONV_W), F32),
                        pltpu.VMEM((tile, d), BF16),
                        pltpu.VMEM((tile, POOL_W), BF16),
                        pltpu.VMEM((d // V7X_MXU_DIM, tile, V7X_MXU_DIM), F32)],
        compiler_params=_params(2),
        name="mixer",
    )(x, attn, up, up, up, gc, gc, gc, inv_cnt, mod, *weights)


def _ffn_kernel(x_ref, mod_ref, gpre_ref, gpost_ref, w1_ref, w2_ref, out_ref):
    d = D_MODEL
    x = x_ref[...]
    hb = _modulated_rmsnorm(x, gpre_ref[...], mod_ref[:, 3 * d:4 * d], mod_ref[:, 4 * d:5 * d]).astype(BF16)
    y = None
    for lo, hi in FF_CHUNKS:
        a = _dot(hb, w1_ref[:, lo:hi])
        b = _dot(hb, w1_ref[:, D_FF + lo:D_FF + hi])
        part = _dot((a * _sigmoid(a) * b).astype(BF16), w2_ref[lo:hi, :])
        y = part if y is None else y + part
    yn = y * lax.rsqrt(jnp.mean(y * y, axis=-1, keepdims=True) + EPS) * gpost_ref[...]
    out_ref[...] = x + mod_ref[:, 5 * d:6 * d] * yn


def _ffn(x, mod, mod_row, g_pre, g_post, w1, w2, *, tile):
    bn, ln, d = x.shape
    if mod_row is None:
        mod_map = lambda b, j: (b, 0, 0)
    else:
        mod_map = lambda b, j: (mod_row, 0, 0)
    tok = pl.BlockSpec((None, tile, d), lambda b, j: (b, j, 0))
    return pl.pallas_call(
        _ffn_kernel,
        out_shape=jax.ShapeDtypeStruct(x.shape, F32),
        grid=(bn, ln // tile),
        in_specs=[tok, pl.BlockSpec((None, 1, mod.shape[-1]), mod_map),
                  _const_spec(g_pre.shape), _const_spec(g_post.shape),
                  _const_spec(w1.shape), _const_spec(w2.shape)],
        out_specs=tok,
        compiler_params=_params(2),
        name="ffn",
    )(x, mod, g_pre, g_post, w1, w2)


def _column_chunks(w):
    k, n = w.shape
    return w.reshape(k, n // V7X_MXU_DIM, V7X_MXU_DIM).transpose(1, 0, 2)


def _rope_tables(n_tokens):
    t = jnp.arange(n_tokens)
    row = (t // GRID_W).astype(F32)
    col = (t % GRID_W).astype(F32)
    n_freq = HEAD_DIM // 4
    inv = 1.0 / (ROPE_THETA ** (jnp.arange(n_freq, dtype=F32) / n_freq))
    ang_r = row[:, None] * inv[None, :]
    ang_c = col[:, None] * inv[None, :]
    cos = jnp.concatenate([jnp.cos(ang_r)] * 2 + [jnp.cos(ang_c)] * 2, axis=-1)
    sin = jnp.concatenate([-jnp.sin(ang_r), jnp.sin(ang_r), -jnp.sin(ang_c), jnp.sin(ang_c)], axis=-1)
    return jnp.tile(cos, (1, 2)), jnp.tile(sin, (1, 2))


def kernel(x, c, ctx, c_ctx, w_ada, b_ada, g_pre1, g_post1, g_pre2, g_post2, w_in, q_norm, k_norm,
           w_attn_o, pool_w, pool_scale, w_pool_o, conv_dw, conv_b, conv_ln_g, conv_ln_b, w_conv_o,
           w_out, w_ffn_in, w_ffn_out):
    batch, n_lat, d = x.shape
    n_ctx = ctx.shape[1]
    depth = w_ada.shape[0]
    ctx_row = batch

    cc = jnp.zeros((MOD_ROWS, d), F32).at[:batch].set(c).at[ctx_row].set(c_ctx)
    mod_all = _ada(cc, w_ada, b_ada).reshape(depth, MOD_ROWS, 1, 6 * d)

    rope = _rope_tables(n_lat)
    lane_head = jnp.arange(ATTN_W) // HEAD_DIM
    headmat = jnp.where(lane_head[:, None] == lane_head[None, :], 1.0 / HEAD_DIM, 0.0).astype(BF16)

    lat_tile = 512
    ctx_tile = n_ctx
    e = ctx
    for i in range(depth):
        last = i == depth - 1
        mod = mod_all[i]
        row = lambda a: a[i].reshape(1, -1)
        w_in_b = w_in[i].astype(BF16)
        w_proj = w_in_b[:, :C_END]
        q_gain = jnp.tile(q_norm[i], N_Q_HEADS).reshape(1, -1)
        k_gain = jnp.tile(k_norm[i], N_KV_HEADS).reshape(1, -1)
        lw = {"g_pre1": row(g_pre1), "g_post1": row(g_post1), "w_gate": _column_chunks(w_in_b[:, C_END:]),
              "w_attn_o": _column_chunks(w_attn_o[i].astype(BF16)), "pool_w": pool_w[i].astype(BF16),
              "pool_scale": row(pool_scale), "w_pool_o": _column_chunks(w_pool_o[i].astype(BF16)),
              "conv_dw": jnp.repeat(conv_dw[i], V7X_SUBLANES, axis=0), "conv_b": row(conv_b), "conv_ln_g": row(conv_ln_g),
              "conv_ln_b": row(conv_ln_b), "w_conv_o": w_conv_o[i].astype(BF16),
              "w_out": w_out[i].astype(BF16)}
        w1 = w_ffn_in[i].astype(BF16)
        w2 = w_ffn_out[i].astype(BF16)

        if last:
            kc, vc = _in_proj(e, mod, ctx_row, lw["g_pre1"], w_proj[:, Q_END:V_END], headmat, q_gain, k_gain,
                              None, tile=ctx_tile, kv_only=True)
        else:
            qc, kc, vc, upc, gcc = _in_proj(e, mod, ctx_row, lw["g_pre1"], w_proj, headmat, q_gain, k_gain,
                                            None, tile=ctx_tile)
        qx, kx, vx, upx, gcx = _in_proj(x, mod, None, lw["g_pre1"], w_proj, headmat, q_gain, k_gain,
                                        rope, tile=lat_tile)
        ax = _attention(qx, (kc, kx), (vc, vx), rows=512)
        x = _mixer(x, ax, upx, gcx, mod, None, lw, tile=lat_tile)
        if not last:
            ac = _attention(qc, (kc,), (vc,), rows=ctx_tile)
            e = _mixer(e, ac, upc, gcc, mod, ctx_row, lw, tile=ctx_tile)

        x = _ffn(x, mod, None, row(g_pre2), row(g_post2), w1, w2, tile=lat_tile)
        if not last:
            e = _ffn(e, mod, ctx_row, row(g_pre2), row(g_post2), w1, w2, tile=ctx_tile)
    return x
```

```python
import functools

import jax
import jax.numpy as jnp
from jax import lax
from jax.experimental import pallas as pl
from jax.experimental.pallas import tpu as pltpu

D_MODEL = 1024
HEAD_DIM = 64
N_Q_HEADS = 8
N_KV_HEADS = 2
Q_GROUP = N_Q_HEADS // N_KV_HEADS
ATTN_W = N_Q_HEADS * HEAD_DIM
KV_W = N_KV_HEADS * HEAD_DIM
GROUP_W = Q_GROUP * HEAD_DIM
POOL_WINDOWS = (2, 4, 8, 16)
POOL_W = D_MODEL // 2
POOL_GROUP_W = POOL_W // len(POOL_WINDOWS)
CONV_W = D_MODEL // 2
CONV_K = 31
GRID_W = 64
ROPE_THETA = 10000.0
EPS = 1e-6
LOG2_E = 1.4426950408889634
Q_END = ATTN_W
K_END = Q_END + KV_W
V_END = K_END + KV_W
P_END = V_END + POOL_W
C_END = P_END + 2 * CONV_W
D_FF = -(-8 * D_MODEL // (3 * 256)) * 256
FF_CHUNKS = ((0, 1024), (1024, 2048), (2048, D_FF))

V7X_LANES = 128
V7X_SUBLANES = 8
V7X_MXU_DIM = 256
POOL_HALO = 8
POOL_SLAB_PAD = 128
CONV_HALO = 16
CONV_ROWS = 32
MOD_ROWS = 24
VMEM_LIMIT = 56 * 1024 * 1024

F32 = jnp.float32
BF16 = jnp.bfloat16


def _const_spec(shape):
    return pl.BlockSpec(shape, lambda *_: (0,) * len(shape), pipeline_mode=pl.Buffered(1))


def _params(n_axes, flags=None):
    return pltpu.CompilerParams(dimension_semantics=("arbitrary",) * n_axes,
                                vmem_limit_bytes=VMEM_LIMIT, flags=flags)


def _dot(a, b):
    return jnp.dot(a, b, preferred_element_type=F32)


def _sigmoid(z):
    return 0.5 * jnp.tanh(0.5 * z) + 0.5


def _modulated_rmsnorm(x, g, shift, scale):
    y = x * lax.rsqrt(jnp.mean(x * x, axis=-1, keepdims=True) + EPS)
    return (y * g) * (1.0 + scale) + shift


def _ada_kernel(c_ref, w_ref, b_ref, o_ref):
    c = c_ref[...]
    s = c * jax.nn.sigmoid(c)
    o_ref[...] = jnp.dot(s, w_ref[...], preferred_element_type=F32,
                         precision=lax.Precision.HIGHEST) + b_ref[...]


def _ada(cc, w_ada, b_ada):
    depth, d, n = w_ada.shape
    bn = 1536
    return pl.pallas_call(
        _ada_kernel,
        out_shape=jax.ShapeDtypeStruct((depth, MOD_ROWS, n), F32),
        grid=(depth, n // bn),
        in_specs=[pl.BlockSpec((MOD_ROWS, d), lambda l, j: (0, 0)),
                  pl.BlockSpec((None, d, bn), lambda l, j: (l, 0, j)),
                  pl.BlockSpec((None, 1, bn), lambda l, j: (l, 0, j))],
        out_specs=pl.BlockSpec((None, MOD_ROWS, bn), lambda l, j: (l, 0, j)),
        compiler_params=_params(2),
        name="adaln",
    )(cc, w_ada, b_ada.reshape(depth, 1, n))


def _head_rms(u, headmat, gain):
    ms = _dot((u * u).astype(BF16), headmat)
    return u * lax.rsqrt(ms + EPS) * gain


def _rope(u, cos, sin_signed):
    lane = lax.broadcasted_iota(jnp.int32, u.shape, 1)
    first_half = (lane % 32) < 16
    partner = jnp.where(first_half,
                        pltpu.roll(u, V7X_LANES - 16, 1),
                        pltpu.roll(u, 16, 1))
    return u * cos + partner * sin_signed


def _replicate_heads(a):
    lane = lax.broadcasted_iota(jnp.int32, a.shape, 1)
    low = lane < HEAD_DIM
    r = pltpu.roll(a, HEAD_DIM, 1)
    return jnp.where(low, a, r), jnp.where(low, r, a)


def _in_proj_kernel(*refs, rope, kv_only):
    it = iter(refs)
    x_ref, mod_ref, gpre_ref, w_ref, hm_ref, qg_ref, kg_ref = (next(it) for _ in range(7))
    cos_ref = sin_ref = None
    if rope:
        cos_ref, sin_ref = next(it), next(it)
    if kv_only:
        k_out, v_out = next(it), next(it)
    else:
        q_out, k_out, v_out, up_out, gc_out = (next(it) for _ in range(5))

    d = D_MODEL
    h = _modulated_rmsnorm(x_ref[...], gpre_ref[...], mod_ref[:, 0:d], mod_ref[:, d:2 * d])
    hb = h.astype(BF16)

    if kv_only:
        ukv = _dot(hb, w_ref[...])
    else:
        ukv = _dot(hb, w_ref[:, Q_END:V_END])
    k = _head_rms(ukv[:, :KV_W], hm_ref[0:KV_W, 0:KV_W], kg_ref[...])
    if rope:
        k = _rope(k, cos_ref[...], sin_ref[...])
    h0, h1 = _replicate_heads(k)
    k_out[:, 0:GROUP_W] = jnp.concatenate([h0, h0], axis=1).astype(BF16)
    k_out[:, GROUP_W:2 * GROUP_W] = jnp.concatenate([h1, h1], axis=1).astype(BF16)
    v = ukv[:, KV_W:]
    low = lax.broadcasted_iota(jnp.int32, v.shape, 1) < HEAD_DIM
    vr = pltpu.roll(v, HEAD_DIM, 1)
    patterns = (jnp.where(low, v, 1.0), jnp.where(low, 1.0, vr),
                jnp.where(low, vr, 1.0), jnp.where(low, 1.0, v))
    for i, pat in enumerate(patterns):
        v_out[i] = jnp.concatenate([pat, pat], axis=1).astype(BF16)
    if kv_only:
        return

    q = _head_rms(_dot(hb, w_ref[:, 0:Q_END]), hm_ref[...], qg_ref[...])
    for c in range(ATTN_W // V7X_LANES):
        sl = slice(c * V7X_LANES, (c + 1) * V7X_LANES)
        qc = q[:, sl]
        if rope:
            qc = _rope(qc, cos_ref[...], sin_ref[...])
        q_out[:, sl] = (qc * (HEAD_DIM ** -0.5 * LOG2_E)).astype(BF16)

    up_out[...] = _dot(hb, w_ref[:, V_END:P_END])
    uc = _dot(hb, w_ref[:, P_END:C_END])
    gc_out[...] = uc[:, :CONV_W] * _sigmoid(uc[:, CONV_W:])


def _in_proj(x, mod, mod_row, g_pre, w, headmat, q_gain, k_gain, rope_tabs, *, tile, kv_only=False):
    bn, ln, d = x.shape
    rope = rope_tabs is not None
    if mod_row is None:
        mod_map = lambda b, j: (b, 0, 0)
    else:
        mod_map = lambda b, j: (mod_row, 0, 0)
    in_specs = [pl.BlockSpec((None, tile, d), lambda b, j: (b, j, 0)),
                pl.BlockSpec((None, 1, mod.shape[-1]), mod_map),
                _const_spec(g_pre.shape), _const_spec(w.shape), _const_spec(headmat.shape),
                _const_spec(q_gain.shape), _const_spec(k_gain.shape)]
    args = [x, mod, g_pre, w, headmat, q_gain, k_gain]
    if rope:
        in_specs += [pl.BlockSpec((tile, V7X_LANES), lambda b, j: (j, 0))] * 2
        args += list(rope_tabs)

    def tok_spec(width):
        return pl.BlockSpec((None, tile, width), lambda b, j: (b, j, 0))

    k_rep = jax.ShapeDtypeStruct((bn, ln, 2 * GROUP_W), BF16)
    v_rep = jax.ShapeDtypeStruct((bn, 2 * N_KV_HEADS, ln, GROUP_W), BF16)
    v_spec = pl.BlockSpec((None, 2 * N_KV_HEADS, tile, GROUP_W), lambda b, j: (b, 0, j, 0))
    if kv_only:
        out_shape = (k_rep, v_rep)
        out_specs = (tok_spec(2 * GROUP_W), v_spec)
    else:
        out_shape = (jax.ShapeDtypeStruct((bn, ln, ATTN_W), BF16), k_rep, v_rep,
                     jax.ShapeDtypeStruct((bn, ln, POOL_W), F32),
                     jax.ShapeDtypeStruct((bn, ln, CONV_W), F32))
        out_specs = (tok_spec(ATTN_W), tok_spec(2 * GROUP_W), v_spec,
                     tok_spec(POOL_W), tok_spec(CONV_W))
    return pl.pallas_call(
        functools.partial(_in_proj_kernel, rope=rope, kv_only=kv_only),
        out_shape=out_shape, grid=(bn, ln // tile), in_specs=in_specs, out_specs=out_specs,
        compiler_params=_params(2),
        name="in_proj_kv" if kv_only else "in_proj",
    )(*args)


def _attn_kernel(*refs, n_src, rows):
    q_ref = refs[0]
    k_refs = refs[1:1 + n_src]
    v_refs = refs[1 + n_src:1 + 2 * n_src]
    o_ref = refs[1 + 2 * n_src]
    acc_ref, s_a, s_b, m_a, m_b = refs[2 + 2 * n_src:]
    n_chunks = q_ref.shape[0] // rows
    head_of_lane = lax.broadcasted_iota(jnp.int32, (rows, GROUP_W), 1) // HEAD_DIM
    k_widths = [k.shape[0] for k in k_refs]

    def chunk_rows(r):
        if isinstance(r, int):
            return pl.ds(r * rows, rows)
        return pl.ds(pl.multiple_of(r * rows, rows), rows)

    def key_blocks():
        col = 0
        for src, width in enumerate(k_widths):
            for c0 in range(0, width, V7X_MXU_DIM):
                yield src, c0, col + c0
            col += width

    def scores(r, g, s_ref, m_ref):
        q = q_ref[chunk_rows(r), :]
        qg = jnp.where(head_of_lane == g, q, jnp.zeros_like(q))
        m = None
        for src, c0, col in key_blocks():
            s = lax.dot_general(qg, k_refs[src][c0:c0 + V7X_MXU_DIM, :], (((1,), (1,)), ((), ())),
                                preferred_element_type=F32)
            s_ref[:, col:col + V7X_MXU_DIM] = s
            blk = jnp.maximum(s[:, :V7X_LANES], s[:, V7X_LANES:])
            m = blk if m is None else jnp.maximum(m, blk)
        m_ref[...] = jnp.broadcast_to(jnp.max(m, axis=-1, keepdims=True), m_ref.shape)

    def softmax_pv(r, g, s_ref, m_ref):
        m = m_ref[...]
        lane_pattern = g % 2
        res = None
        for src, c0, col in key_blocks():
            p = [jnp.exp2(s_ref[:, col + i * V7X_LANES:col + (i + 1) * V7X_LANES] - m).astype(BF16)
                 for i in range(V7X_MXU_DIM // V7X_LANES)]
            part = _dot(jnp.concatenate(p, axis=1), v_refs[src][lane_pattern, c0:c0 + V7X_MXU_DIM, :])
            res = part if res is None else res + part
        norm = [res[:, h * V7X_LANES:(h + 1) * V7X_LANES] for h in range(GROUP_W // V7X_LANES)]
        norm = jnp.concatenate([n / pltpu.roll(n, HEAD_DIM, 1) for n in norm], axis=1)
        mine = head_of_lane == g
        if g == 0:
            acc_ref[...] = jnp.where(mine, norm, 0.0)
        elif g < Q_GROUP - 1:
            acc_ref[...] = jnp.where(mine, norm, acc_ref[...])
        else:
            o_ref[chunk_rows(r), :] = jnp.where(mine, norm, acc_ref[...]).astype(o_ref.dtype)

    def chunk(r, carry, prefetch_next=True):
        bufs = ((s_a, m_a), (s_b, m_b))
        for g in range(Q_GROUP):
            if g + 1 < Q_GROUP:
                scores(r, g + 1, *bufs[(g + 1) % 2])
            elif prefetch_next:
                scores(r + 1, 0, *bufs[0])
            softmax_pv(r, g, *bufs[g % 2])
        return carry

    scores(0, 0, s_a, m_a)
    for r in range(n_chunks):
        chunk(r, 0, prefetch_next=r + 1 < n_chunks)


def _attention(q, k_srcs, v_srcs, *, rows):
    bn, ln, _ = q.shape
    n_src = len(k_srcs)
    n_keys = sum(k.shape[1] for k in k_srcs)
    q_spec = pl.BlockSpec((None, ln, GROUP_W), lambda b, h: (b, 0, h))
    kv_specs = ([pl.BlockSpec((None, k.shape[1], GROUP_W), lambda b, h: (b, 0, h)) for k in k_srcs]
                + [pl.BlockSpec((None, 2, v.shape[2], GROUP_W), lambda b, h: (b, h, 0, 0)) for v in v_srcs])
    return pl.pallas_call(
        functools.partial(_attn_kernel, n_src=n_src, rows=rows),
        out_shape=jax.ShapeDtypeStruct((bn, ln, ATTN_W), BF16),
        grid=(bn, N_KV_HEADS),
        in_specs=[q_spec] + kv_specs, out_specs=q_spec,
        scratch_shapes=[pltpu.VMEM((rows, GROUP_W), F32),
                        pltpu.VMEM((rows, n_keys), F32), pltpu.VMEM((rows, n_keys), F32),
                        pltpu.VMEM((rows, V7X_LANES), F32), pltpu.VMEM((rows, V7X_LANES), F32)],
        compiler_params=_params(2),
        name="attention",
    )(q, *k_srcs, *v_srcs)


def _mixer_kernel(x_ref, o_ref, up_ref, up_prev_ref, up_next_ref, gc_ref, gc_prev_ref, gc_next_ref,
                  icnt_ref, mod_ref, band_ref, gpre_ref, gpost_ref, wgate_ref, wao_ref, poolw_ref, pscale_ref,
                  wpo_ref, cdw_ref, cb_ref, lng_ref, lnb_ref, wco_ref, wout_ref,
                  out_ref, ppad_ref, gpad_ref, gsh_ref, cacc_ref):
    tile = x_ref.shape[0]
    d = D_MODEL
    j = pl.program_id(1)
    first = j == 0
    last = j == pl.num_programs(1) - 1

    gpad_ref[0:CONV_HALO, :] = jnp.where(first, 0.0, gc_prev_ref[...])
    gpad_ref[CONV_HALO:CONV_HALO + tile, :] = gc_ref[...]
    gpad_ref[CONV_HALO + tile:, :] = jnp.where(last, 0.0, gc_next_ref[...])
    tap0 = CONV_HALO - CONV_K // 2
    shifted_rows = gsh_ref.shape[1]
    for r in range(V7X_SUBLANES):
        gsh_ref[r] = gpad_ref[r:r + shifted_rows, :]
    groups = CONV_ROWS // V7X_SUBLANES
    for r0 in range(0, tile, CONV_ROWS):
        acc = jnp.broadcast_to(cb_ref[...], (groups, V7X_SUBLANES, CONV_W))
        for k in range(CONV_K):
            phase = (tap0 + k) % V7X_SUBLANES
            base = (tap0 + k) - phase
            g = gsh_ref[phase, r0 + base:r0 + base + CONV_ROWS, :].reshape(groups, V7X_SUBLANES, CONV_W)
            acc = acc + cdw_ref[k * V7X_SUBLANES:(k + 1) * V7X_SUBLANES, :][None] * g
        cacc_ref[r0:r0 + CONV_ROWS, :] = acc.reshape(CONV_ROWS, CONV_W)
    yc = cacc_ref[...]
    cen = yc - jnp.mean(yc, axis=-1, keepdims=True)
    var = jnp.mean(cen * cen, axis=-1, keepdims=True)
    yc = cen * lax.rsqrt(var + EPS) * lng_ref[...] + lnb_ref[...]
    yc = (yc * _sigmoid(yc)).astype(BF16)

    x = x_ref[...]
    hb = _modulated_rmsnorm(x, gpre_ref[...], mod_ref[:, 0:d], mod_ref[:, d:2 * d]).astype(BF16)

    merged = _sigmoid(_dot(hb, wgate_ref[:, 0:d])) * _dot(o_ref[...], wao_ref[...])

    ppad_ref[0:POOL_HALO, :] = jnp.where(first, 0.0, up_prev_ref[...])
    ppad_ref[POOL_HALO:POOL_HALO + tile, :] = up_ref[...]
    ppad_ref[POOL_HALO + tile:2 * POOL_HALO + tile, :] = jnp.where(last, 0.0, up_next_ref[...])
    ppad_ref[2 * POOL_HALO + tile:, :] = jnp.zeros((ppad_ref.shape[0] - 2 * POOL_HALO - tile, POOL_W), F32)
    pb = ppad_ref[...].astype(BF16)
    pooled = []
    for gi in range(len(POOL_WINDOWS)):
        cols = slice(gi * POOL_GROUP_W, (gi + 1) * POOL_GROUP_W)
        dg = _dot(band_ref[gi], pb[:, cols]) * icnt_ref[:, cols] - up_ref[:, cols]
        pooled.append(_dot(dg.astype(BF16), poolw_ref[gi]))
    yb = jnp.concatenate(pooled, axis=-1) * pscale_ref[...]
    merged = merged + _sigmoid(_dot(hb, wgate_ref[:, d:2 * d])) * _dot(yb.astype(BF16), wpo_ref[...])
    merged = merged + _sigmoid(_dot(hb, wgate_ref[:, 2 * d:3 * d])) * _dot(yc, wco_ref[...])

    y = _dot(merged.astype(BF16), wout_ref[...])
    yn = y * lax.rsqrt(jnp.mean(y * y, axis=-1, keepdims=True) + EPS) * gpost_ref[...]
    out_ref[...] = x + mod_ref[:, 2 * d:3 * d] * yn


def _pool_tables(n_tokens, tile):
    slab = tile + POOL_SLAB_PAD
    t = jnp.arange(tile)[:, None]
    i = jnp.arange(slab)[None, :] - POOL_HALO
    band = jnp.stack([(i >= t - w // 2) & (i < t + w // 2) for w in POOL_WINDOWS]).astype(BF16)
    pos = jnp.arange(n_tokens)
    inv = [1.0 / (jnp.minimum(pos + w // 2, n_tokens) - jnp.maximum(pos - w // 2, 0)).astype(F32)
           for w in POOL_WINDOWS]
    return band, jnp.repeat(jnp.stack(inv, axis=-1), POOL_GROUP_W, axis=-1)


def _mixer(x, attn, up, gc, mod, mod_row, lw, *, tile):
    bn, ln, d = x.shape
    n_tiles = ln // tile
    if mod_row is None:
        mod_map = lambda b, j: (b, 0, 0)
    else:
        mod_map = lambda b, j: (mod_row, 0, 0)

    def tok_spec(width):
        return pl.BlockSpec((None, tile, width), lambda b, j: (b, j, 0))

    def halo_specs(width, halo):
        per_tile = tile // halo
        n_blocks = ln // halo
        prev = pl.BlockSpec((None, halo, width), lambda b, j: (b, jnp.maximum(j * per_tile - 1, 0), 0))
        nxt = pl.BlockSpec((None, halo, width),
                           lambda b, j: (b, jnp.minimum((j + 1) * per_tile, n_blocks - 1), 0))
        return [tok_spec(width), prev, nxt]

    band, inv_cnt = _pool_tables(ln, tile)
    weights = [band, lw["g_pre1"], lw["g_post1"], lw["w_gate"], lw["w_attn_o"], lw["pool_w"], lw["pool_scale"],
               lw["w_pool_o"], lw["conv_dw"], lw["conv_b"], lw["conv_ln_g"], lw["conv_ln_b"],
               lw["w_conv_o"], lw["w_out"]]
    in_specs = ([tok_spec(d), tok_spec(ATTN_W)] + halo_specs(POOL_W, POOL_HALO) + halo_specs(CONV_W, CONV_HALO)
                + [pl.BlockSpec((tile, POOL_W), lambda b, j: (j, 0)),
                   pl.BlockSpec((None, 1, mod.shape[-1]), mod_map)] + [_const_spec(w.shape) for w in weights])
    return pl.pallas_call(
        _mixer_kernel,
        out_shape=jax.ShapeDtypeStruct(x.shape, F32),
        grid=(bn, n_tiles), in_specs=in_specs, out_specs=tok_spec(d),
        scratch_shapes=[pltpu.VMEM((band.shape[-1], POOL_W), F32),
                        pltpu.VMEM((tile + 2 * CONV_HALO, CONV_W), F32),
                        pltpu.VMEM((V7X_SUBLANES, tile + 2 * CONV_HALO - V7X_SUBLANES, CONV_W), F32),
                        pltpu.VMEM((tile, CONV_W), F32)],
        compiler_params=_params(2),
        name="mixer",
    )(x, attn, up, up, up, gc, gc, gc, inv_cnt, mod, *weights)


def _ffn_kernel(x_ref, mod_ref, gpre_ref, gpost_ref, w1_ref, w2_ref, out_ref):
    d = D_MODEL
    x = x_ref[...]
    hb = _modulated_rmsnorm(x, gpre_ref[...], mod_ref[:, 3 * d:4 * d], mod_ref[:, 4 * d:5 * d]).astype(BF16)
    y = None
    for lo, hi in FF_CHUNKS:
        a = _dot(hb, w1_ref[:, lo:hi])
        b = _dot(hb, w1_ref[:, D_FF + lo:D_FF + hi])
        part = _dot((a * _sigmoid(a) * b).astype(BF16), w2_ref[lo:hi, :])
        y = part if y is None else y + part
    yn = y * lax.rsqrt(jnp.mean(y * y, axis=-1, keepdims=True) + EPS) * gpost_ref[...]
    out_ref[...] = x + mod_ref[:, 5 * d:6 * d] * yn


def _ffn(x, mod, mod_row, g_pre, g_post, w1, w2, *, tile):
    bn, ln, d = x.shape
    if mod_row is None:
        mod_map = lambda b, j: (b, 0, 0)
    else:
        mod_map = lambda b, j: (mod_row, 0, 0)
    tok = pl.BlockSpec((None, tile, d), lambda b, j: (b, j, 0))
    return pl.pallas_call(
        _ffn_kernel,
        out_shape=jax.ShapeDtypeStruct(x.shape, F32),
        grid=(bn, ln // tile),
        in_specs=[tok, pl.BlockSpec((None, 1, mod.shape[-1]), mod_map),
                  _const_spec(g_pre.shape), _const_spec(g_post.shape),
                  _const_spec(w1.shape), _const_spec(w2.shape)],
        out_specs=tok,
        compiler_params=_params(2),
        name="ffn",
    )(x, mod, g_pre, g_post, w1, w2)


def _rope_tables(n_tokens):
    t = jnp.arange(n_tokens)
    row = (t // GRID_W).astype(F32)
    col = (t % GRID_W).astype(F32)
    n_freq = HEAD_DIM // 4
    inv = 1.0 / (ROPE_THETA ** (jnp.arange(n_freq, dtype=F32) / n_freq))
    ang_r = row[:, None] * inv[None, :]
    ang_c = col[:, None] * inv[None, :]
    cos = jnp.concatenate([jnp.cos(ang_r)] * 2 + [jnp.cos(ang_c)] * 2, axis=-1)
    sin = jnp.concatenate([-jnp.sin(ang_r), jnp.sin(ang_r), -jnp.sin(ang_c), jnp.sin(ang_c)], axis=-1)
    return jnp.tile(cos, (1, 2)), jnp.tile(sin, (1, 2))


def kernel(x, c, ctx, c_ctx, w_ada, b_ada, g_pre1, g_post1, g_pre2, g_post2, w_in, q_norm, k_norm,
           w_attn_o, pool_w, pool_scale, w_pool_o, conv_dw, conv_b, conv_ln_g, conv_ln_b, w_conv_o,
           w_out, w_ffn_in, w_ffn_out):
    batch, n_lat, d = x.shape
    n_ctx = ctx.shape[1]
    depth = w_ada.shape[0]
    ctx_row = batch

    cc = jnp.zeros((MOD_ROWS, d), F32).at[:batch].set(c).at[ctx_row].set(c_ctx)
    mod_all = _ada(cc, w_ada, b_ada).reshape(depth, MOD_ROWS, 1, 6 * d)

    rope = _rope_tables(n_lat)
    lane_head = jnp.arange(ATTN_W) // HEAD_DIM
    headmat = jnp.where(lane_head[:, None] == lane_head[None, :], 1.0 / HEAD_DIM, 0.0).astype(BF16)

    lat_tile = 512
    ctx_tile = n_ctx
    e = ctx
    for i in range(depth):
        last = i == depth - 1
        mod = mod_all[i]
        row = lambda a: a[i].reshape(1, -1)
        w_in_b = w_in[i].astype(BF16)
        w_proj = w_in_b[:, :C_END]
        q_gain = jnp.tile(q_norm[i], N_Q_HEADS).reshape(1, -1)
        k_gain = jnp.tile(k_norm[i], N_KV_HEADS).reshape(1, -1)
        lw = {"g_pre1": row(g_pre1), "g_post1": row(g_post1), "w_gate": w_in_b[:, C_END:],
              "w_attn_o": w_attn_o[i].astype(BF16), "pool_w": pool_w[i].astype(BF16),
              "pool_scale": row(pool_scale), "w_pool_o": w_pool_o[i].astype(BF16),
              "conv_dw": jnp.repeat(conv_dw[i], V7X_SUBLANES, axis=0), "conv_b": row(conv_b), "conv_ln_g": row(conv_ln_g),
              "conv_ln_b": row(conv_ln_b), "w_conv_o": w_conv_o[i].astype(BF16),
              "w_out": w_out[i].astype(BF16)}
        w1 = w_ffn_in[i].astype(BF16)
        w2 = w_ffn_out[i].astype(BF16)

        if last:
            kc, vc = _in_proj(e, mod, ctx_row, lw["g_pre1"], w_proj[:, Q_END:V_END], headmat, q_gain, k_gain,
                              None, tile=ctx_tile, kv_only=True)
        else:
            qc, kc, vc, upc, gcc = _in_proj(e, mod, ctx_row, lw["g_pre1"], w_proj, headmat, q_gain, k_gain,
                                            None, tile=ctx_tile)
        qx, kx, vx, upx, gcx = _in_proj(x, mod, None, lw["g_pre1"], w_proj, headmat, q_gain, k_gain,
                                        rope, tile=lat_tile)
        ax = _attention(qx, (kc, kx), (vc, vx), rows=512)
        x = _mixer(x, ax, upx, gcx, mod, None, lw, tile=lat_tile)
        if not last:
            ac = _attention(qc, (kc,), (vc,), rows=ctx_tile)
            e = _mixer(e, ac, upc, gcc, mod, ctx_row, lw, tile=ctx_tile)

        x = _ffn(x, mod, None, row(g_pre2), row(g_post2), w1, w2, tile=lat_tile)
        if not last:
            e = _ffn(e, mod, ctx_row, row(g_pre2), row(g_post2), w1, w2, tile=ctx_tile)
    return x
```

```python
import functools

import jax
import jax.numpy as jnp
from jax import lax
from jax.experimental import pallas as pl
from jax.experimental.pallas import tpu as pltpu

D_MODEL = 1024
HEAD_DIM = 64
N_Q_HEADS = 8
N_KV_HEADS = 2
Q_GROUP = N_Q_HEADS // N_KV_HEADS
ATTN_W = N_Q_HEADS * HEAD_DIM
KV_W = N_KV_HEADS * HEAD_DIM
GROUP_W = Q_GROUP * HEAD_DIM
POOL_WINDOWS = (2, 4, 8, 16)
POOL_W = D_MODEL // 2
POOL_GROUP_W = POOL_W // len(POOL_WINDOWS)
CONV_W = D_MODEL // 2
CONV_K = 31
GRID_W = 64
ROPE_THETA = 10000.0
EPS = 1e-6
LOG2_E = 1.4426950408889634
Q_END = ATTN_W
K_END = Q_END + KV_W
V_END = K_END + KV_W
P_END = V_END + POOL_W
C_END = P_END + 2 * CONV_W
D_FF = -(-8 * D_MODEL // (3 * 256)) * 256
FF_CHUNKS = ((0, 1024), (1024, 2048), (2048, D_FF))

V7X_LANES = 128
V7X_SUBLANES = 8
V7X_MXU_DIM = 256
POOL_HALO = 8
POOL_SLAB_PAD = 128
CONV_HALO = 16
CONV_ROWS = 32
IN_PROJ_SUB_ROWS = 512
FFN_SUB_ROWS = 512
MOD_ROWS = 24
VMEM_LIMIT = 56 * 1024 * 1024

F32 = jnp.float32
BF16 = jnp.bfloat16


def _const_spec(shape):
    return pl.BlockSpec(shape, lambda *_: (0,) * len(shape), pipeline_mode=pl.Buffered(1))


def _params(n_axes, flags=None):
    return pltpu.CompilerParams(dimension_semantics=("arbitrary",) * n_axes,
                                vmem_limit_bytes=VMEM_LIMIT, flags=flags)


def _dot(a, b):
    return jnp.dot(a, b, preferred_element_type=F32)


def _gate(half_z, half_y):
    return (jnp.tanh(half_z) + 1.0) * half_y


def _modulated_rmsnorm(x, g, shift, scale):
    y = x * lax.rsqrt(jnp.mean(x * x, axis=-1, keepdims=True) + EPS)
    return (y * g) * (1.0 + scale) + shift


def _ada_kernel(c_ref, w_ref, b_ref, o_ref):
    c = c_ref[...]
    s = c * jax.nn.sigmoid(c)
    o_ref[...] = jnp.dot(s, w_ref[...], preferred_element_type=F32,
                         precision=lax.Precision.HIGHEST) + b_ref[...]


def _ada(cc, w_ada, b_ada):
    depth, d, n = w_ada.shape
    bn = 1536
    return pl.pallas_call(
        _ada_kernel,
        out_shape=jax.ShapeDtypeStruct((depth, MOD_ROWS, n), F32),
        grid=(depth, n // bn),
        in_specs=[pl.BlockSpec((MOD_ROWS, d), lambda l, j: (0, 0)),
                  pl.BlockSpec((None, d, bn), lambda l, j: (l, 0, j)),
                  pl.BlockSpec((None, 1, bn), lambda l, j: (l, 0, j))],
        out_specs=pl.BlockSpec((None, MOD_ROWS, bn), lambda l, j: (l, 0, j)),
        compiler_params=_params(2),
        name="adaln",
    )(cc, w_ada, b_ada.reshape(depth, 1, n))


def _head_rms(u, headmat, gain):
    ms = _dot((u * u).astype(BF16), headmat)
    return u * lax.rsqrt(ms + EPS) * gain


def _rope(u, cos, sin_signed):
    lane = lax.broadcasted_iota(jnp.int32, u.shape, 1)
    first_half = (lane % 32) < 16
    partner = jnp.where(first_half,
                        pltpu.roll(u, V7X_LANES - 16, 1),
                        pltpu.roll(u, 16, 1))
    return u * cos + partner * sin_signed


def _replicate_heads(a):
    lane = lax.broadcasted_iota(jnp.int32, a.shape, 1)
    low = lane < HEAD_DIM
    r = pltpu.roll(a, HEAD_DIM, 1)
    return jnp.where(low, a, r), jnp.where(low, r, a)


def _in_proj_kernel(*refs, rope, kv_only, sub_rows):
    it = iter(refs)
    x_ref, mod_ref, gpre_ref, w_ref, hm_ref, qg_ref, kg_ref = (next(it) for _ in range(7))
    cos_ref = sin_ref = None
    if rope:
        cos_ref, sin_ref = next(it), next(it)
    if kv_only:
        k_out, v_out = next(it), next(it)
    else:
        q_out, k_out, v_out, up_out, gc_out = (next(it) for _ in range(5))

    d = D_MODEL
    for r0 in range(0, x_ref.shape[0], sub_rows):
        rows = slice(r0, r0 + sub_rows)
        h = _modulated_rmsnorm(x_ref[rows, :], gpre_ref[...], mod_ref[:, 0:d], mod_ref[:, d:2 * d])
        hb = h.astype(BF16)

        if kv_only:
            ukv = _dot(hb, w_ref[...])
        else:
            ukv = _dot(hb, w_ref[:, Q_END:V_END])
        k = _head_rms(ukv[:, :KV_W], hm_ref[0:KV_W, 0:KV_W], kg_ref[...])
        if rope:
            k = _rope(k, cos_ref[rows, :], sin_ref[rows, :])
        h0, h1 = _replicate_heads(k)
        k_out[rows, 0:GROUP_W] = jnp.concatenate([h0, h0], axis=1).astype(BF16)
        k_out[rows, GROUP_W:2 * GROUP_W] = jnp.concatenate([h1, h1], axis=1).astype(BF16)
        v = ukv[:, KV_W:]
        low = lax.broadcasted_iota(jnp.int32, v.shape, 1) < HEAD_DIM
        vr = pltpu.roll(v, HEAD_DIM, 1)
        patterns = (jnp.where(low, v, 1.0), jnp.where(low, 1.0, vr),
                    jnp.where(low, vr, 1.0), jnp.where(low, 1.0, v))
        for i, pat in enumerate(patterns):
            v_out[i, rows, :] = jnp.concatenate([pat, pat], axis=1).astype(BF16)
        if kv_only:
            continue

        q = _head_rms(_dot(hb, w_ref[:, 0:Q_END]), hm_ref[...], qg_ref[...])
        for c in range(ATTN_W // V7X_LANES):
            sl = slice(c * V7X_LANES, (c + 1) * V7X_LANES)
            qc = q[:, sl]
            if rope:
                qc = _rope(qc, cos_ref[rows, :], sin_ref[rows, :])
            q_out[rows, sl] = (qc * (HEAD_DIM ** -0.5 * LOG2_E)).astype(BF16)

        up_out[rows, :] = _dot(hb, w_ref[:, V_END:P_END])
        uc = _dot(hb, w_ref[:, P_END:C_END])
        gc_out[rows, :] = _gate(uc[:, CONV_W:], uc[:, :CONV_W])


def _in_proj(x, mod, mod_row, g_pre, w, headmat, q_gain, k_gain, rope_tabs, *, tile, kv_only=False):
    bn, ln, d = x.shape
    rope = rope_tabs is not None
    if mod_row is None:
        mod_map = lambda b, j: (b, 0, 0)
    else:
        mod_map = lambda b, j: (mod_row, 0, 0)
    in_specs = [pl.BlockSpec((None, tile, d), lambda b, j: (b, j, 0)),
                pl.BlockSpec((None, 1, mod.shape[-1]), mod_map),
                _const_spec(g_pre.shape), _const_spec(w.shape), _const_spec(headmat.shape),
                _const_spec(q_gain.shape), _const_spec(k_gain.shape)]
    args = [x, mod, g_pre, w, headmat, q_gain, k_gain]
    if rope:
        in_specs += [pl.BlockSpec((tile, V7X_LANES), lambda b, j: (j, 0))] * 2
        args += list(rope_tabs)

    def tok_spec(width):
        return pl.BlockSpec((None, tile, width), lambda b, j: (b, j, 0))

    k_rep = jax.ShapeDtypeStruct((bn, ln, 2 * GROUP_W), BF16)
    v_rep = jax.ShapeDtypeStruct((bn, 2 * N_KV_HEADS, ln, GROUP_W), BF16)
    v_spec = pl.BlockSpec((None, 2 * N_KV_HEADS, tile, GROUP_W), lambda b, j: (b, 0, j, 0))
    if kv_only:
        out_shape = (k_rep, v_rep)
        out_specs = (tok_spec(2 * GROUP_W), v_spec)
    else:
        out_shape = (jax.ShapeDtypeStruct((bn, ln, ATTN_W), BF16), k_rep, v_rep,
                     jax.ShapeDtypeStruct((bn, ln, POOL_W), F32),
                     jax.ShapeDtypeStruct((bn, ln, CONV_W), F32))
        out_specs = (tok_spec(ATTN_W), tok_spec(2 * GROUP_W), v_spec,
                     tok_spec(POOL_W), tok_spec(CONV_W))
    return pl.pallas_call(
        functools.partial(_in_proj_kernel, rope=rope, kv_only=kv_only, sub_rows=min(tile, IN_PROJ_SUB_ROWS)),
        out_shape=out_shape, grid=(bn, ln // tile), in_specs=in_specs, out_specs=out_specs,
        compiler_params=_params(2),
        name="in_proj_kv" if kv_only else "in_proj",
    )(*args)


def _attn_kernel(*refs, n_src, rows):
    q_ref = refs[0]
    k_refs = refs[1:1 + n_src]
    v_refs = refs[1 + n_src:1 + 2 * n_src]
    o_ref = refs[1 + 2 * n_src]
    acc_ref, s_a, s_b, m_a, m_b = refs[2 + 2 * n_src:]
    n_chunks = q_ref.shape[0] // rows
    head_of_lane = lax.broadcasted_iota(jnp.int32, (rows, GROUP_W), 1) // HEAD_DIM
    k_widths = [k.shape[0] for k in k_refs]

    def chunk_rows(r):
        if isinstance(r, int):
            return pl.ds(r * rows, rows)
        return pl.ds(pl.multiple_of(r * rows, rows), rows)

    def key_blocks():
        col = 0
        for src, width in enumerate(k_widths):
            for c0 in range(0, width, V7X_MXU_DIM):
                yield src, c0, col + c0
            col += width

    def scores(r, g, s_ref, m_ref):
        q = q_ref[chunk_rows(r), :]
        qg = jnp.where(head_of_lane == g, q, jnp.zeros_like(q))
        m = None
        for src, c0, col in key_blocks():
            s = lax.dot_general(qg, k_refs[src][c0:c0 + V7X_MXU_DIM, :], (((1,), (1,)), ((), ())),
                                preferred_element_type=F32)
            s_ref[:, col:col + V7X_MXU_DIM] = s
            blk = jnp.maximum(s[:, :V7X_LANES], s[:, V7X_LANES:])
            m = blk if m is None else jnp.maximum(m, blk)
        m_ref[...] = jnp.broadcast_to(jnp.max(m, axis=-1, keepdims=True), m_ref.shape)

    def softmax_pv(r, g, s_ref, m_ref):
        m = m_ref[...]
        lane_pattern = g % 2
        res = None
        for src, c0, col in key_blocks():
            p = [jnp.exp2(s_ref[:, col + i * V7X_LANES:col + (i + 1) * V7X_LANES] - m).astype(BF16)
                 for i in range(V7X_MXU_DIM // V7X_LANES)]
            part = _dot(jnp.concatenate(p, axis=1), v_refs[src][lane_pattern, c0:c0 + V7X_MXU_DIM, :])
            res = part if res is None else res + part
        norm = [res[:, h * V7X_LANES:(h + 1) * V7X_LANES] for h in range(GROUP_W // V7X_LANES)]
        norm = jnp.concatenate([n / pltpu.roll(n, HEAD_DIM, 1) for n in norm], axis=1)
        mine = head_of_lane == g
        if g == 0:
            acc_ref[...] = jnp.where(mine, norm, 0.0)
        elif g < Q_GROUP - 1:
            acc_ref[...] = jnp.where(mine, norm, acc_ref[...])
        else:
            o_ref[chunk_rows(r), :] = jnp.where(mine, norm, acc_ref[...]).astype(o_ref.dtype)

    def chunk(r, carry, prefetch_next=True):
        bufs = ((s_a, m_a), (s_b, m_b))
        for g in range(Q_GROUP):
            if g + 1 < Q_GROUP:
                scores(r, g + 1, *bufs[(g + 1) % 2])
            elif prefetch_next:
                scores(r + 1, 0, *bufs[0])
            softmax_pv(r, g, *bufs[g % 2])
        return carry

    scores(0, 0, s_a, m_a)
    for r in range(n_chunks):
        chunk(r, 0, prefetch_next=r + 1 < n_chunks)


def _attention(q, k_srcs, v_srcs, *, rows):
    bn, ln, _ = q.shape
    n_src = len(k_srcs)
    n_keys = sum(k.shape[1] for k in k_srcs)
    q_spec = pl.BlockSpec((None, ln, GROUP_W), lambda b, h: (b, 0, h))
    kv_specs = ([pl.BlockSpec((None, k.shape[1], GROUP_W), lambda b, h: (b, 0, h)) for k in k_srcs]
                + [pl.BlockSpec((None, 2, v.shape[2], GROUP_W), lambda b, h: (b, h, 0, 0)) for v in v_srcs])
    return pl.pallas_call(
        functools.partial(_attn_kernel, n_src=n_src, rows=rows),
        out_shape=jax.ShapeDtypeStruct((bn, ln, ATTN_W), BF16),
        grid=(bn, N_KV_HEADS),
        in_specs=[q_spec] + kv_specs, out_specs=q_spec,
        scratch_shapes=[pltpu.VMEM((rows, GROUP_W), F32),
                        pltpu.VMEM((rows, n_keys), F32), pltpu.VMEM((rows, n_keys), F32),
                        pltpu.VMEM((rows, V7X_LANES), F32), pltpu.VMEM((rows, V7X_LANES), F32)],
        compiler_params=_params(2),
        name="attention",
    )(q, *k_srcs, *v_srcs)


def _mixer_kernel(x_ref, o_ref, up_ref, up_prev_ref, up_next_ref, gc_ref, gc_prev_ref, gc_next_ref,
                  icnt_ref, mod_ref, band_ref, gpre_ref, gpost_ref, wgate_ref, wao_ref, poolw_ref, pscale_ref,
                  wpo_ref, cdw_ref, cb_ref, lng_ref, lnb_ref, wco_ref, wout_ref,
                  out_ref, ppad_ref, gpad_ref, gsh_ref, cacc_ref):
    tile = x_ref.shape[0]
    d = D_MODEL
    j = pl.program_id(1)
    first = j == 0
    last = j == pl.num_programs(1) - 1

    gpad_ref[0:CONV_HALO, :] = jnp.where(first, 0.0, gc_prev_ref[...])
    gpad_ref[CONV_HALO:CONV_HALO + tile, :] = gc_ref[...]
    gpad_ref[CONV_HALO + tile:, :] = jnp.where(last, 0.0, gc_next_ref[...])
    tap0 = CONV_HALO - CONV_K // 2
    shifted_rows = gsh_ref.shape[1]
    for r in range(V7X_SUBLANES):
        gsh_ref[r] = gpad_ref[r:r + shifted_rows, :]
    groups = CONV_ROWS // V7X_SUBLANES
    for r0 in range(0, tile, CONV_ROWS):
        acc = jnp.broadcast_to(cb_ref[...], (groups, V7X_SUBLANES, CONV_W))
        for k in range(CONV_K):
            phase = (tap0 + k) % V7X_SUBLANES
            base = (tap0 + k) - phase
            g = gsh_ref[phase, r0 + base:r0 + base + CONV_ROWS, :].reshape(groups, V7X_SUBLANES, CONV_W)
            acc = acc + cdw_ref[k * V7X_SUBLANES:(k + 1) * V7X_SUBLANES, :][None] * g
        cacc_ref[r0:r0 + CONV_ROWS, :] = acc.reshape(CONV_ROWS, CONV_W)
    yc = cacc_ref[...]
    cen = yc - jnp.mean(yc, axis=-1, keepdims=True)
    var = jnp.mean(cen * cen, axis=-1, keepdims=True)
    yc = cen * lax.rsqrt(var + EPS) * lng_ref[...] + lnb_ref[...]
    yc = _gate(yc, yc).astype(BF16)

    x = x_ref[...]
    hb = _modulated_rmsnorm(x, gpre_ref[...], mod_ref[:, 0:d], mod_ref[:, d:2 * d]).astype(BF16)

    merged = _gate(_dot(hb, wgate_ref[:, 0:d]), _dot(o_ref[...], wao_ref[...]))

    ppad_ref[0:POOL_HALO, :] = jnp.where(first, 0.0, up_prev_ref[...])
    ppad_ref[POOL_HALO:POOL_HALO + tile, :] = up_ref[...]
    ppad_ref[POOL_HALO + tile:2 * POOL_HALO + tile, :] = jnp.where(last, 0.0, up_next_ref[...])
    ppad_ref[2 * POOL_HALO + tile:, :] = jnp.zeros((ppad_ref.shape[0] - 2 * POOL_HALO - tile, POOL_W), F32)
    pb = ppad_ref[...].astype(BF16)
    pooled = []
    for gi in range(len(POOL_WINDOWS)):
        cols = slice(gi * POOL_GROUP_W, (gi + 1) * POOL_GROUP_W)
        dg = _dot(band_ref[gi], pb[:, cols]) * icnt_ref[:, cols] - up_ref[:, cols]
        pooled.append(_dot(dg.astype(BF16), poolw_ref[gi]))
    yb = jnp.concatenate(pooled, axis=-1) * pscale_ref[...]
    merged = merged + _gate(_dot(hb, wgate_ref[:, d:2 * d]), _dot(yb.astype(BF16), wpo_ref[...]))
    merged = merged + _gate(_dot(hb, wgate_ref[:, 2 * d:3 * d]), _dot(yc, wco_ref[...]))

    y = _dot(merged.astype(BF16), wout_ref[...])
    yn = y * lax.rsqrt(jnp.mean(y * y, axis=-1, keepdims=True) + EPS) * gpost_ref[...]
    out_ref[...] = x + mod_ref[:, 2 * d:3 * d] * yn


def _pool_tables(n_tokens, tile):
    slab = tile + POOL_SLAB_PAD
    t = jnp.arange(tile)[:, None]
    i = jnp.arange(slab)[None, :] - POOL_HALO
    band = jnp.stack([(i >= t - w // 2) & (i < t + w // 2) for w in POOL_WINDOWS]).astype(BF16)
    pos = jnp.arange(n_tokens)
    inv = [1.0 / (jnp.minimum(pos + w // 2, n_tokens) - jnp.maximum(pos - w // 2, 0)).astype(F32)
           for w in POOL_WINDOWS]
    return band, jnp.repeat(jnp.stack(inv, axis=-1), POOL_GROUP_W, axis=-1)


def _mixer(x, attn, up, gc, mod, mod_row, lw, *, tile):
    bn, ln, d = x.shape
    n_tiles = ln // tile
    if mod_row is None:
        mod_map = lambda b, j: (b, 0, 0)
    else:
        mod_map = lambda b, j: (mod_row, 0, 0)

    def tok_spec(width):
        return pl.BlockSpec((None, tile, width), lambda b, j: (b, j, 0))

    def halo_specs(width, halo):
        per_tile = tile // halo
        n_blocks = ln // halo
        prev = pl.BlockSpec((None, halo, width), lambda b, j: (b, jnp.maximum(j * per_tile - 1, 0), 0))
        nxt = pl.BlockSpec((None, halo, width),
                           lambda b, j: (b, jnp.minimum((j + 1) * per_tile, n_blocks - 1), 0))
        return [tok_spec(width), prev, nxt]

    band, inv_cnt = _pool_tables(ln, tile)
    weights = [band, lw["g_pre1"], lw["g_post1"], lw["w_gate"], lw["w_attn_o"], lw["pool_w"], lw["pool_scale"],
               lw["w_pool_o"], lw["conv_dw"], lw["conv_b"], lw["conv_ln_g"], lw["conv_ln_b"],
               lw["w_conv_o"], lw["w_out"]]
    in_specs = ([tok_spec(d), tok_spec(ATTN_W)] + halo_specs(POOL_W, POOL_HALO) + halo_specs(CONV_W, CONV_HALO)
                + [pl.BlockSpec((tile, POOL_W), lambda b, j: (j, 0)),
                   pl.BlockSpec((None, 1, mod.shape[-1]), mod_map)] + [_const_spec(w.shape) for w in weights])
    return pl.pallas_call(
        _mixer_kernel,
        out_shape=jax.ShapeDtypeStruct(x.shape, F32),
        grid=(bn, n_tiles), in_specs=in_specs, out_specs=tok_spec(d),
        scratch_shapes=[pltpu.VMEM((band.shape[-1], POOL_W), F32),
                        pltpu.VMEM((tile + 2 * CONV_HALO, CONV_W), F32),
                        pltpu.VMEM((V7X_SUBLANES, tile + 2 * CONV_HALO - V7X_SUBLANES, CONV_W), F32),
                        pltpu.VMEM((tile, CONV_W), F32)],
        compiler_params=_params(2),
        name="mixer",
    )(x, attn, up, up, up, gc, gc, gc, inv_cnt, mod, *weights)


def _ffn_kernel(x_ref, mod_ref, gpre_ref, gpost_ref, w1_ref, w2_ref, out_ref, *, sub_rows):
    d = D_MODEL
    for r0 in range(0, x_ref.shape[0], sub_rows):
        rows = slice(r0, r0 + sub_rows)
        x = x_ref[rows, :]
        hb = _modulated_rmsnorm(x, gpre_ref[...], mod_ref[:, 3 * d:4 * d], mod_ref[:, 4 * d:5 * d]).astype(BF16)
        y = None
        for lo, hi in FF_CHUNKS:
            a = _dot(hb, w1_ref[:, lo:hi])
            b = _dot(hb, w1_ref[:, D_FF + lo:D_FF + hi])
            part = _dot((_gate(a, a) * b).astype(BF16), w2_ref[lo:hi, :])
            y = part if y is None else y + part
        yn = y * lax.rsqrt(jnp.mean(y * y, axis=-1, keepdims=True) + EPS) * gpost_ref[...]
        out_ref[rows, :] = x + mod_ref[:, 5 * d:6 * d] * yn


def _ffn(x, mod, mod_row, g_pre, g_post, w1, w2, *, tile):
    bn, ln, d = x.shape
    if mod_row is None:
        mod_map = lambda b, j: (b, 0, 0)
    else:
        mod_map = lambda b, j: (mod_row, 0, 0)
    tok = pl.BlockSpec((None, tile, d), lambda b, j: (b, j, 0))
    return pl.pallas_call(
        functools.partial(_ffn_kernel, sub_rows=min(tile, FFN_SUB_ROWS)),
        out_shape=jax.ShapeDtypeStruct(x.shape, F32),
        grid=(bn, ln // tile),
        in_specs=[tok, pl.BlockSpec((None, 1, mod.shape[-1]), mod_map),
                  _const_spec(g_pre.shape), _const_spec(g_post.shape),
                  _const_spec(w1.shape), _const_spec(w2.shape)],
        out_specs=tok,
        compiler_params=_params(2),
        name="ffn",
    )(x, mod, g_pre, g_post, w1, w2)


def _rope_tables(n_tokens):
    t = jnp.arange(n_tokens)
    row = (t // GRID_W).astype(F32)
    col = (t % GRID_W).astype(F32)
    n_freq = HEAD_DIM // 4
    inv = 1.0 / (ROPE_THETA ** (jnp.arange(n_freq, dtype=F32) / n_freq))
    ang_r = row[:, None] * inv[None, :]
    ang_c = col[:, None] * inv[None, :]
    cos = jnp.concatenate([jnp.cos(ang_r)] * 2 + [jnp.cos(ang_c)] * 2, axis=-1)
    sin = jnp.concatenate([-jnp.sin(ang_r), jnp.sin(ang_r), -jnp.sin(ang_c), jnp.sin(ang_c)], axis=-1)
    return jnp.tile(cos, (1, 2)), jnp.tile(sin, (1, 2))


def kernel(x, c, ctx, c_ctx, w_ada, b_ada, g_pre1, g_post1, g_pre2, g_post2, w_in, q_norm, k_norm,
           w_attn_o, pool_w, pool_scale, w_pool_o, conv_dw, conv_b, conv_ln_g, conv_ln_b, w_conv_o,
           w_out, w_ffn_in, w_ffn_out):
    batch, n_lat, d = x.shape
    n_ctx = ctx.shape[1]
    depth = w_ada.shape[0]
    ctx_row = batch

    cc = jnp.zeros((MOD_ROWS, d), F32).at[:batch].set(c).at[ctx_row].set(c_ctx)
    mod_all = _ada(cc, w_ada, b_ada).reshape(depth, MOD_ROWS, 1, 6 * d)

    rope = _rope_tables(n_lat)
    lane_head = jnp.arange(ATTN_W) // HEAD_DIM
    headmat = jnp.where(lane_head[:, None] == lane_head[None, :], 1.0 / HEAD_DIM, 0.0).astype(BF16)

    lat_tile = 512
    ctx_tile = n_ctx
    e = ctx
    for i in range(depth):
        last = i == depth - 1
        mod = mod_all[i]
        row = lambda a: a[i].reshape(1, -1)
        half_from = lambda n, start: jnp.where(jnp.arange(n) >= start, 0.5, 1.0).astype(F32)
        w_in_b = (w_in[i] * half_from(w_in.shape[-1], P_END)).astype(BF16)
        w_proj = w_in_b[:, :C_END]
        q_gain = jnp.tile(q_norm[i], N_Q_HEADS).reshape(1, -1)
        k_gain = jnp.tile(k_norm[i], N_KV_HEADS).reshape(1, -1)
        lw = {"g_pre1": row(g_pre1), "g_post1": row(g_post1), "w_gate": w_in_b[:, C_END:],
              "w_attn_o": (0.5 * w_attn_o[i]).astype(BF16), "pool_w": pool_w[i].astype(BF16),
              "pool_scale": row(pool_scale), "w_pool_o": (0.5 * w_pool_o[i]).astype(BF16),
              "conv_dw": jnp.repeat(conv_dw[i], V7X_SUBLANES, axis=0), "conv_b": row(conv_b),
              "conv_ln_g": 0.5 * row(conv_ln_g), "conv_ln_b": 0.5 * row(conv_ln_b),
              "w_conv_o": (0.5 * w_conv_o[i]).astype(BF16), "w_out": w_out[i].astype(BF16)}
        w1 = (w_ffn_in[i] * (1.5 - half_from(w_ffn_in.shape[-1], D_FF))).astype(BF16)
        w2 = w_ffn_out[i].astype(BF16)

        if last:
            kc, vc = _in_proj(e, mod, ctx_row, lw["g_pre1"], w_proj[:, Q_END:V_END], headmat, q_gain, k_gain,
                              None, tile=ctx_tile, kv_only=True)
        else:
            qc, kc, vc, upc, gcc = _in_proj(e, mod, ctx_row, lw["g_pre1"], w_proj, headmat, q_gain, k_gain,
                                            None, tile=ctx_tile)
        qx, kx, vx, upx, gcx = _in_proj(x, mod, None, lw["g_pre1"], w_proj, headmat, q_gain, k_gain,
                                        rope, tile=2 * IN_PROJ_SUB_ROWS)
        ax = _attention(qx, (kc, kx), (vc, vx), rows=512)
        x = _mixer(x, ax, upx, gcx, mod, None, lw, tile=lat_tile)
        if not last:
            ac = _attention(qc, (kc,), (vc,), rows=ctx_tile)
            e = _mixer(e, ac, upc, gcc, mod, ctx_row, lw, tile=ctx_tile)

        x = _ffn(x, mod, None, row(g_pre2), row(g_post2), w1, w2, tile=2 * FFN_SUB_ROWS)
        if not last:
            e = _ffn(e, mod, ctx_row, row(g_pre2), row(g_post2), w1, w2, tile=ctx_tile)
    return x
```

```python
import functools

import jax
import jax.numpy as jnp
from jax import lax
from jax.experimental import pallas as pl
from jax.experimental.pallas import tpu as pltpu

D_MODEL = 1024
HEAD_DIM = 64
N_Q_HEADS = 8
N_KV_HEADS = 2
Q_GROUP = N_Q_HEADS // N_KV_HEADS
ATTN_W = N_Q_HEADS * HEAD_DIM
KV_W = N_KV_HEADS * HEAD_DIM
GROUP_W = Q_GROUP * HEAD_DIM
POOL_WINDOWS = (2, 4, 8, 16)
POOL_W = D_MODEL // 2
POOL_GROUP_W = POOL_W // len(POOL_WINDOWS)
CONV_W = D_MODEL // 2
CONV_K = 31
GRID_W = 64
ROPE_THETA = 10000.0
EPS = 1e-6
LOG2_E = 1.4426950408889634
Q_END = ATTN_W
K_END = Q_END + KV_W
V_END = K_END + KV_W
P_END = V_END + POOL_W
C_END = P_END + 2 * CONV_W
D_FF = -(-8 * D_MODEL // (3 * 256)) * 256
FF_CHUNKS = ((0, 1024), (1024, 2048), (2048, D_FF))

V7X_LANES = 128
V7X_SUBLANES = 8
V7X_MXU_DIM = 256
POOL_HALO = 8
POOL_SLAB_PAD = 128
CONV_HALO = 16
CONV_ROWS = 32
IN_PROJ_SUB_ROWS = 512
FFN_SUB_ROWS = 512
MOD_ROWS = 24
VMEM_LIMIT = 56 * 1024 * 1024

F32 = jnp.float32
BF16 = jnp.bfloat16


def _const_spec(shape):
    return pl.BlockSpec(shape, lambda *_: (0,) * len(shape), pipeline_mode=pl.Buffered(1))


def _params(n_axes, flags=None):
    return pltpu.CompilerParams(dimension_semantics=("arbitrary",) * n_axes,
                                vmem_limit_bytes=VMEM_LIMIT, flags=flags)


def _dot(a, b):
    return jnp.dot(a, b, preferred_element_type=F32)


def _gate(half_z, half_y):
    return (jnp.tanh(half_z) + 1.0) * half_y


def _modulated_rmsnorm(x, g, shift, scale):
    y = x * lax.rsqrt(jnp.mean(x * x, axis=-1, keepdims=True) + EPS)
    return (y * g) * (1.0 + scale) + shift


def _ada_kernel(c_ref, w_ref, b_ref, o_ref):
    c = c_ref[...]
    s = c * jax.nn.sigmoid(c)
    o_ref[...] = jnp.dot(s, w_ref[...], preferred_element_type=F32,
                         precision=lax.Precision.HIGHEST) + b_ref[...]


def _ada(cc, w_ada, b_ada):
    depth, d, n = w_ada.shape
    bn = 1536
    return pl.pallas_call(
        _ada_kernel,
        out_shape=jax.ShapeDtypeStruct((depth, MOD_ROWS, n), F32),
        grid=(depth, n // bn),
        in_specs=[pl.BlockSpec((MOD_ROWS, d), lambda l, j: (0, 0)),
                  pl.BlockSpec((None, d, bn), lambda l, j: (l, 0, j)),
                  pl.BlockSpec((None, 1, bn), lambda l, j: (l, 0, j))],
        out_specs=pl.BlockSpec((None, MOD_ROWS, bn), lambda l, j: (l, 0, j)),
        compiler_params=_params(2),
        name="adaln",
    )(cc, w_ada, b_ada.reshape(depth, 1, n))


def _head_rms(u, headmat, gain):
    ms = _dot((u * u).astype(BF16), headmat)
    return u * lax.rsqrt(ms + EPS) * gain


def _rope(u, cos, sin_signed):
    lane = lax.broadcasted_iota(jnp.int32, u.shape, 1)
    first_half = (lane % 32) < 16
    partner = jnp.where(first_half,
                        pltpu.roll(u, V7X_LANES - 16, 1),
                        pltpu.roll(u, 16, 1))
    return u * cos + partner * sin_signed


def _replicate_heads(a):
    lane = lax.broadcasted_iota(jnp.int32, a.shape, 1)
    low = lane < HEAD_DIM
    r = pltpu.roll(a, HEAD_DIM, 1)
    return jnp.where(low, a, r), jnp.where(low, r, a)


def _in_proj_kernel(*refs, rope, kv_only, sub_rows):
    it = iter(refs)
    x_ref, mod_ref, gpre_ref, w_ref, hm_ref, qg_ref, kg_ref = (next(it) for _ in range(7))
    cos_ref = sin_ref = None
    if rope:
        cos_ref, sin_ref = next(it), next(it)
    if kv_only:
        k_out, v_out = next(it), next(it)
    else:
        q_out, k_out, v_out, up_out, gc_out = (next(it) for _ in range(5))

    d = D_MODEL
    for r0 in range(0, x_ref.shape[0], sub_rows):
        rows = slice(r0, r0 + sub_rows)
        h = _modulated_rmsnorm(x_ref[rows, :], gpre_ref[...], mod_ref[:, 0:d], mod_ref[:, d:2 * d])
        hb = h.astype(BF16)

        if kv_only:
            ukv = _dot(hb, w_ref[...])
        else:
            ukv = _dot(hb, w_ref[:, Q_END:V_END])
        k = _head_rms(ukv[:, :KV_W], hm_ref[0:KV_W, 0:KV_W], kg_ref[...])
        if rope:
            k = _rope(k, cos_ref[rows, :], sin_ref[rows, :])
        h0, h1 = _replicate_heads(k)
        k_out[rows, 0:GROUP_W] = jnp.concatenate([h0, h0], axis=1).astype(BF16)
        k_out[rows, GROUP_W:2 * GROUP_W] = jnp.concatenate([h1, h1], axis=1).astype(BF16)
        v = ukv[:, KV_W:]
        low = lax.broadcasted_iota(jnp.int32, v.shape, 1) < HEAD_DIM
        vr = pltpu.roll(v, HEAD_DIM, 1)
        patterns = (jnp.where(low, v, 1.0), jnp.where(low, 1.0, vr),
                    jnp.where(low, vr, 1.0), jnp.where(low, 1.0, v))
        for i, pat in enumerate(patterns):
            v_out[i, rows, :] = jnp.concatenate([pat, pat], axis=1).astype(BF16)
        if kv_only:
            continue

        q = _head_rms(_dot(hb, w_ref[:, 0:Q_END]), hm_ref[...], qg_ref[...])
        for c in range(ATTN_W // V7X_LANES):
            sl = slice(c * V7X_LANES, (c + 1) * V7X_LANES)
            qc = q[:, sl]
            if rope:
                qc = _rope(qc, cos_ref[rows, :], sin_ref[rows, :])
            q_out[rows, sl] = (qc * (HEAD_DIM ** -0.5 * LOG2_E)).astype(BF16)

        up_out[rows, :] = _dot(hb, w_ref[:, V_END:P_END])
        uc = _dot(hb, w_ref[:, P_END:C_END])
        gc_out[rows, :] = _gate(uc[:, CONV_W:], uc[:, :CONV_W])


def _in_proj(x, mod, mod_row, g_pre, w, headmat, q_gain, k_gain, rope_tabs, *, tile, kv_only=False):
    bn, ln, d = x.shape
    rope = rope_tabs is not None
    if mod_row is None:
        mod_map = lambda b, j: (b, 0, 0)
    else:
        mod_map = lambda b, j: (mod_row, 0, 0)
    in_specs = [pl.BlockSpec((None, tile, d), lambda b, j: (b, j, 0)),
                pl.BlockSpec((None, 1, mod.shape[-1]), mod_map),
                _const_spec(g_pre.shape), _const_spec(w.shape), _const_spec(headmat.shape),
                _const_spec(q_gain.shape), _const_spec(k_gain.shape)]
    args = [x, mod, g_pre, w, headmat, q_gain, k_gain]
    if rope:
        in_specs += [pl.BlockSpec((tile, V7X_LANES), lambda b, j: (j, 0))] * 2
        args += list(rope_tabs)

    def tok_spec(width):
        return pl.BlockSpec((None, tile, width), lambda b, j: (b, j, 0))

    k_rep = jax.ShapeDtypeStruct((bn, ln, 2 * GROUP_W), BF16)
    v_rep = jax.ShapeDtypeStruct((bn, 2 * N_KV_HEADS, ln, GROUP_W), BF16)
    v_spec = pl.BlockSpec((None, 2 * N_KV_HEADS, tile, GROUP_W), lambda b, j: (b, 0, j, 0))
    if kv_only:
        out_shape = (k_rep, v_rep)
        out_specs = (tok_spec(2 * GROUP_W), v_spec)
    else:
        out_shape = (jax.ShapeDtypeStruct((bn, ln, ATTN_W), BF16), k_rep, v_rep,
                     jax.ShapeDtypeStruct((bn, ln, POOL_W), F32),
                     jax.ShapeDtypeStruct((bn, ln, CONV_W), F32))
        out_specs = (tok_spec(ATTN_W), tok_spec(2 * GROUP_W), v_spec,
                     tok_spec(POOL_W), tok_spec(CONV_W))
    return pl.pallas_call(
        functools.partial(_in_proj_kernel, rope=rope, kv_only=kv_only, sub_rows=min(tile, IN_PROJ_SUB_ROWS)),
        out_shape=out_shape, grid=(bn, ln // tile), in_specs=in_specs, out_specs=out_specs,
        compiler_params=_params(2),
        name="in_proj_kv" if kv_only else "in_proj",
    )(*args)


def _attn_kernel(*refs, n_src, rows):
    q_ref = refs[0]
    k_refs = refs[1:1 + n_src]
    v_refs = refs[1 + n_src:1 + 2 * n_src]
    o_ref = refs[1 + 2 * n_src]
    acc_ref, s_a, s_b, m_a, m_b = refs[2 + 2 * n_src:]
    n_chunks = q_ref.shape[0] // rows
    head_of_lane = lax.broadcasted_iota(jnp.int32, (rows, GROUP_W), 1) // HEAD_DIM
    k_widths = [k.shape[0] for k in k_refs]

    def chunk_rows(r):
        if isinstance(r, int):
            return pl.ds(r * rows, rows)
        return pl.ds(pl.multiple_of(r * rows, rows), rows)

    def key_blocks():
        col = 0
        for src, width in enumerate(k_widths):
            for c0 in range(0, width, V7X_MXU_DIM):
                yield src, c0, col + c0
            col += width

    def scores(r, g, s_ref, m_ref):
        q = q_ref[chunk_rows(r), :]
        qg = jnp.where(head_of_lane == g, q, jnp.zeros_like(q))
        m = None
        for src, c0, col in key_blocks():
            s = lax.dot_general(qg, k_refs[src][c0:c0 + V7X_MXU_DIM, :], (((1,), (1,)), ((), ())),
                                preferred_element_type=F32)
            s_ref[:, col:col + V7X_MXU_DIM] = s
            blk = jnp.maximum(s[:, :V7X_LANES], s[:, V7X_LANES:])
            m = blk if m is None else jnp.maximum(m, blk)
        m_ref[...] = jnp.broadcast_to(jnp.max(m, axis=-1, keepdims=True), m_ref.shape)

    def softmax_pv(r, g, s_ref, m_ref):
        m = m_ref[...]
        lane_pattern = g % 2
        res = None
        for src, c0, col in key_blocks():
            p = [jnp.exp2(s_ref[:, col + i * V7X_LANES:col + (i + 1) * V7X_LANES] - m).astype(BF16)
                 for i in range(V7X_MXU_DIM // V7X_LANES)]
            part = _dot(jnp.concatenate(p, axis=1), v_refs[src][lane_pattern, c0:c0 + V7X_MXU_DIM, :])
            res = part if res is None else res + part
        norm = [res[:, h * V7X_LANES:(h + 1) * V7X_LANES] for h in range(GROUP_W // V7X_LANES)]
        norm = jnp.concatenate([n / pltpu.roll(n, HEAD_DIM, 1) for n in norm], axis=1)
        mine = head_of_lane == g
        if g == 0:
            acc_ref[...] = jnp.where(mine, norm, 0.0)
        elif g < Q_GROUP - 1:
            acc_ref[...] = jnp.where(mine, norm, acc_ref[...])
        else:
            o_ref[chunk_rows(r), :] = jnp.where(mine, norm, acc_ref[...]).astype(o_ref.dtype)

    def chunk(r, carry, prefetch_next=True):
        bufs = ((s_a, m_a), (s_b, m_b))
        for g in range(Q_GROUP):
            if g + 1 < Q_GROUP:
                scores(r, g + 1, *bufs[(g + 1) % 2])
            elif prefetch_next:
                scores(r + 1, 0, *bufs[0])
            softmax_pv(r, g, *bufs[g % 2])
        return carry

    scores(0, 0, s_a, m_a)
    for r in range(n_chunks):
        chunk(r, 0, prefetch_next=r + 1 < n_chunks)


def _attention(q, k_srcs, v_srcs, *, rows):
    bn, ln, _ = q.shape
    n_src = len(k_srcs)
    n_keys = sum(k.shape[1] for k in k_srcs)
    q_spec = pl.BlockSpec((None, ln, GROUP_W), lambda b, h: (b, 0, h))
    kv_specs = ([pl.BlockSpec((None, k.shape[1], GROUP_W), lambda b, h: (b, 0, h)) for k in k_srcs]
                + [pl.BlockSpec((None, 2, v.shape[2], GROUP_W), lambda b, h: (b, h, 0, 0)) for v in v_srcs])
    return pl.pallas_call(
        functools.partial(_attn_kernel, n_src=n_src, rows=rows),
        out_shape=jax.ShapeDtypeStruct((bn, ln, ATTN_W), BF16),
        grid=(bn, N_KV_HEADS),
        in_specs=[q_spec] + kv_specs, out_specs=q_spec,
        scratch_shapes=[pltpu.VMEM((rows, GROUP_W), F32),
                        pltpu.VMEM((rows, n_keys), F32), pltpu.VMEM((rows, n_keys), F32),
                        pltpu.VMEM((rows, V7X_LANES), F32), pltpu.VMEM((rows, V7X_LANES), F32)],
        compiler_params=_params(2),
        name="attention",
    )(q, *k_srcs, *v_srcs)


def _mixer_kernel(x_ref, o_ref, up_ref, up_prev_ref, up_next_ref, gc_ref, gc_prev_ref, gc_next_ref,
                  icnt_ref, mod_ref, band_ref, gpre_ref, gpost_ref, wgate_ref, wao_ref, poolw_ref, pscale_ref,
                  wpo_ref, cdw_ref, cb_ref, lng_ref, lnb_ref, wco_ref, wout_ref,
                  out_ref, ppad_ref, gpad_ref, gsh_ref, cacc_ref):
    tile = x_ref.shape[0]
    d = D_MODEL
    j = pl.program_id(1)
    first = j == 0
    last = j == pl.num_programs(1) - 1

    gpad_ref[0:CONV_HALO, :] = jnp.where(first, 0.0, gc_prev_ref[...])
    gpad_ref[CONV_HALO:CONV_HALO + tile, :] = gc_ref[...]
    gpad_ref[CONV_HALO + tile:, :] = jnp.where(last, 0.0, gc_next_ref[...])
    tap0 = CONV_HALO - CONV_K // 2
    shifted_rows = gsh_ref.shape[1]
    for r in range(V7X_SUBLANES):
        gsh_ref[r] = gpad_ref[r:r + shifted_rows, :]
    groups = CONV_ROWS // V7X_SUBLANES
    for r0 in range(0, tile, CONV_ROWS):
        acc = jnp.broadcast_to(cb_ref[...], (groups, V7X_SUBLANES, CONV_W))
        for k in range(CONV_K):
            phase = (tap0 + k) % V7X_SUBLANES
            base = (tap0 + k) - phase
            g = gsh_ref[phase, r0 + base:r0 + base + CONV_ROWS, :].reshape(groups, V7X_SUBLANES, CONV_W)
            acc = acc + cdw_ref[k * V7X_SUBLANES:(k + 1) * V7X_SUBLANES, :][None] * g
        cacc_ref[r0:r0 + CONV_ROWS, :] = acc.reshape(CONV_ROWS, CONV_W)
    yc = cacc_ref[...]
    cen = yc - jnp.mean(yc, axis=-1, keepdims=True)
    var = jnp.mean(cen * cen, axis=-1, keepdims=True)
    yc = cen * lax.rsqrt(var + EPS) * lng_ref[...] + lnb_ref[...]
    yc = _gate(yc, yc).astype(BF16)

    x = x_ref[...]
    hb = _modulated_rmsnorm(x, gpre_ref[...], mod_ref[:, 0:d], mod_ref[:, d:2 * d]).astype(BF16)

    ppad_ref[0:POOL_HALO, :] = jnp.where(first, 0.0, up_prev_ref[...])
    ppad_ref[POOL_HALO:POOL_HALO + tile, :] = up_ref[...]
    ppad_ref[POOL_HALO + tile:2 * POOL_HALO + tile, :] = jnp.where(last, 0.0, up_next_ref[...])
    ppad_ref[2 * POOL_HALO + tile:, :] = jnp.zeros((ppad_ref.shape[0] - 2 * POOL_HALO - tile, POOL_W), F32)
    pb = ppad_ref[...].astype(BF16)
    pooled = []
    for gi in range(len(POOL_WINDOWS)):
        cols = slice(gi * POOL_GROUP_W, (gi + 1) * POOL_GROUP_W)
        dg = _dot(band_ref[gi], pb[:, cols]) * icnt_ref[:, cols] - up_ref[:, cols]
        pooled.append(_dot(dg.astype(BF16), poolw_ref[gi]))
    yb = (jnp.concatenate(pooled, axis=-1) * pscale_ref[...]).astype(BF16)

    o = o_ref[...]
    y = None
    for c0 in range(0, d, V7X_MXU_DIM):
        cols = slice(c0, c0 + V7X_MXU_DIM)
        merged = (_gate(_dot(hb, wgate_ref[:, c0:c0 + V7X_MXU_DIM]), _dot(o, wao_ref[:, cols]))
                  + _gate(_dot(hb, wgate_ref[:, d + c0:d + c0 + V7X_MXU_DIM]), _dot(yb, wpo_ref[:, cols]))
                  + _gate(_dot(hb, wgate_ref[:, 2 * d + c0:2 * d + c0 + V7X_MXU_DIM]), _dot(yc, wco_ref[:, cols])))
        part = _dot(merged.astype(BF16), wout_ref[cols, :])
        y = part if y is None else y + part

    yn = y * lax.rsqrt(jnp.mean(y * y, axis=-1, keepdims=True) + EPS) * gpost_ref[...]
    out_ref[...] = x + mod_ref[:, 2 * d:3 * d] * yn


def _pool_tables(n_tokens, tile):
    slab = tile + POOL_SLAB_PAD
    t = jnp.arange(tile)[:, None]
    i = jnp.arange(slab)[None, :] - POOL_HALO
    band = jnp.stack([(i >= t - w // 2) & (i < t + w // 2) for w in POOL_WINDOWS]).astype(BF16)
    pos = jnp.arange(n_tokens)
    inv = [1.0 / (jnp.minimum(pos + w // 2, n_tokens) - jnp.maximum(pos - w // 2, 0)).astype(F32)
           for w in POOL_WINDOWS]
    return band, jnp.repeat(jnp.stack(inv, axis=-1), POOL_GROUP_W, axis=-1)


def _mixer(x, attn, up, gc, mod, mod_row, lw, *, tile):
    bn, ln, d = x.shape
    n_tiles = ln // tile
    if mod_row is None:
        mod_map = lambda b, j: (b, 0, 0)
    else:
        mod_map = lambda b, j: (mod_row, 0, 0)

    def tok_spec(width):
        return pl.BlockSpec((None, tile, width), lambda b, j: (b, j, 0))

    def halo_specs(width, halo):
        per_tile = tile // halo
        n_blocks = ln // halo
        prev = pl.BlockSpec((None, halo, width), lambda b, j: (b, jnp.maximum(j * per_tile - 1, 0), 0))
        nxt = pl.BlockSpec((None, halo, width),
                           lambda b, j: (b, jnp.minimum((j + 1) * per_tile, n_blocks - 1), 0))
        return [tok_spec(width), prev, nxt]

    band, inv_cnt = _pool_tables(ln, tile)
    weights = [band, lw["g_pre1"], lw["g_post1"], lw["w_gate"], lw["w_attn_o"], lw["pool_w"], lw["pool_scale"],
               lw["w_pool_o"], lw["conv_dw"], lw["conv_b"], lw["conv_ln_g"], lw["conv_ln_b"],
               lw["w_conv_o"], lw["w_out"]]
    in_specs = ([tok_spec(d), tok_spec(ATTN_W)] + halo_specs(POOL_W, POOL_HALO) + halo_specs(CONV_W, CONV_HALO)
                + [pl.BlockSpec((tile, POOL_W), lambda b, j: (j, 0)),
                   pl.BlockSpec((None, 1, mod.shape[-1]), mod_map)] + [_const_spec(w.shape) for w in weights])
    return pl.pallas_call(
        _mixer_kernel,
        out_shape=jax.ShapeDtypeStruct(x.shape, F32),
        grid=(bn, n_tiles), in_specs=in_specs, out_specs=tok_spec(d),
        scratch_shapes=[pltpu.VMEM((band.shape[-1], POOL_W), F32),
                        pltpu.VMEM((tile + 2 * CONV_HALO, CONV_W), F32),
                        pltpu.VMEM((V7X_SUBLANES, tile + 2 * CONV_HALO - V7X_SUBLANES, CONV_W), F32),
                        pltpu.VMEM((tile, CONV_W), F32)],
        compiler_params=_params(2),
        name="mixer",
    )(x, attn, up, up, up, gc, gc, gc, inv_cnt, mod, *weights)


def _ffn_kernel(x_ref, mod_ref, gpre_ref, gpost_ref, w1_ref, w2_ref, out_ref, *, sub_rows):
    d = D_MODEL
    for r0 in range(0, x_ref.shape[0], sub_rows):
        rows = slice(r0, r0 + sub_rows)
        x = x_ref[rows, :]
        hb = _modulated_rmsnorm(x, gpre_ref[...], mod_ref[:, 3 * d:4 * d], mod_ref[:, 4 * d:5 * d]).astype(BF16)
        y = None
        for lo, hi in FF_CHUNKS:
            a = _dot(hb, w1_ref[:, lo:hi])
            b = _dot(hb, w1_ref[:, D_FF + lo:D_FF + hi])
            part = _dot((_gate(a, a) * b).astype(BF16), w2_ref[lo:hi, :])
            y = part if y is None else y + part
        yn = y * lax.rsqrt(jnp.mean(y * y, axis=-1, keepdims=True) + EPS) * gpost_ref[...]
        out_ref[rows, :] = x + mod_ref[:, 5 * d:6 * d] * yn


def _ffn(x, mod, mod_row, g_pre, g_post, w1, w2, *, tile):
    bn, ln, d = x.shape
    if mod_row is None:
        mod_map = lambda b, j: (b, 0, 0)
    else:
        mod_map = lambda b, j: (mod_row, 0, 0)
    tok = pl.BlockSpec((None, tile, d), lambda b, j: (b, j, 0))
    return pl.pallas_call(
        functools.partial(_ffn_kernel, sub_rows=min(tile, FFN_SUB_ROWS)),
        out_shape=jax.ShapeDtypeStruct(x.shape, F32),
        grid=(bn, ln // tile),
        in_specs=[tok, pl.BlockSpec((None, 1, mod.shape[-1]), mod_map),
                  _const_spec(g_pre.shape), _const_spec(g_post.shape),
                  _const_spec(w1.shape), _const_spec(w2.shape)],
        out_specs=tok,
        compiler_params=_params(2),
        name="ffn",
    )(x, mod, g_pre, g_post, w1, w2)


def _rope_tables(n_tokens):
    t = jnp.arange(n_tokens)
    row = (t // GRID_W).astype(F32)
    col = (t % GRID_W).astype(F32)
    n_freq = HEAD_DIM // 4
    inv = 1.0 / (ROPE_THETA ** (jnp.arange(n_freq, dtype=F32) / n_freq))
    ang_r = row[:, None] * inv[None, :]
    ang_c = col[:, None] * inv[None, :]
    cos = jnp.concatenate([jnp.cos(ang_r)] * 2 + [jnp.cos(ang_c)] * 2, axis=-1)
    sin = jnp.concatenate([-jnp.sin(ang_r), jnp.sin(ang_r), -jnp.sin(ang_c), jnp.sin(ang_c)], axis=-1)
    return jnp.tile(cos, (1, 2)), jnp.tile(sin, (1, 2))


def kernel(x, c, ctx, c_ctx, w_ada, b_ada, g_pre1, g_post1, g_pre2, g_post2, w_in, q_norm, k_norm,
           w_attn_o, pool_w, pool_scale, w_pool_o, conv_dw, conv_b, conv_ln_g, conv_ln_b, w_conv_o,
           w_out, w_ffn_in, w_ffn_out):
    batch, n_lat, d = x.shape
    n_ctx = ctx.shape[1]
    depth = w_ada.shape[0]
    ctx_row = batch

    cc = jnp.zeros((MOD_ROWS, d), F32).at[:batch].set(c).at[ctx_row].set(c_ctx)
    mod_all = _ada(cc, w_ada, b_ada).reshape(depth, MOD_ROWS, 1, 6 * d)

    rope = _rope_tables(n_lat)
    lane_head = jnp.arange(ATTN_W) // HEAD_DIM
    headmat = jnp.where(lane_head[:, None] == lane_head[None, :], 1.0 / HEAD_DIM, 0.0).astype(BF16)

    lat_tile = 512
    ctx_tile = n_ctx
    e = ctx
    for i in range(depth):
        last = i == depth - 1
        mod = mod_all[i]
        row = lambda a: a[i].reshape(1, -1)
        half_from = lambda n, start: jnp.where(jnp.arange(n) >= start, 0.5, 1.0).astype(F32)
        w_in_b = (w_in[i] * half_from(w_in.shape[-1], P_END)).astype(BF16)
        w_proj = w_in_b[:, :C_END]
        q_gain = jnp.tile(q_norm[i], N_Q_HEADS).reshape(1, -1)
        k_gain = jnp.tile(k_norm[i], N_KV_HEADS).reshape(1, -1)
        lw = {"g_pre1": row(g_pre1), "g_post1": row(g_post1), "w_gate": w_in_b[:, C_END:],
              "w_attn_o": (0.5 * w_attn_o[i]).astype(BF16), "pool_w": pool_w[i].astype(BF16),
              "pool_scale": row(pool_scale), "w_pool_o": (0.5 * w_pool_o[i]).astype(BF16),
              "conv_dw": jnp.repeat(conv_dw[i], V7X_SUBLANES, axis=0), "conv_b": row(conv_b),
              "conv_ln_g": 0.5 * row(conv_ln_g), "conv_ln_b": 0.5 * row(conv_ln_b),
              "w_conv_o": (0.5 * w_conv_o[i]).astype(BF16), "w_out": w_out[i].astype(BF16)}
        w1 = (w_ffn_in[i] * (1.5 - half_from(w_ffn_in.shape[-1], D_FF))).astype(BF16)
        w2 = w_ffn_out[i].astype(BF16)

        if last:
            kc, vc = _in_proj(e, mod, ctx_row, lw["g_pre1"], w_proj[:, Q_END:V_END], headmat, q_gain, k_gain,
                              None, tile=ctx_tile, kv_only=True)
        else:
            qc, kc, vc, upc, gcc = _in_proj(e, mod, ctx_row, lw["g_pre1"], w_proj, headmat, q_gain, k_gain,
                                            None, tile=ctx_tile)
        qx, kx, vx, upx, gcx = _in_proj(x, mod, None, lw["g_pre1"], w_proj, headmat, q_gain, k_gain,
                                        rope, tile=2 * IN_PROJ_SUB_ROWS)
        ax = _attention(qx, (kc, kx), (vc, vx), rows=512)
        x = _mixer(x, ax, upx, gcx, mod, None, lw, tile=lat_tile)
        if not last:
            ac = _attention(qc, (kc,), (vc,), rows=ctx_tile)
            e = _mixer(e, ac, upc, gcc, mod, ctx_row, lw, tile=ctx_tile)

        x = _ffn(x, mod, None, row(g_pre2), row(g_post2), w1, w2, tile=2 * FFN_SUB_ROWS)
        if not last:
            rows_per_call = 2 * FFN_SUB_ROWS
            e = _ffn(e.reshape(-1, rows_per_call, d), mod, ctx_row, row(g_pre2), row(g_post2), w1, w2,
                     tile=rows_per_call).reshape(e.shape)
    return x
```

```python
import functools

import jax
import jax.numpy as jnp
from jax import lax
from jax.experimental import pallas as pl
from jax.experimental.pallas import tpu as pltpu

D_MODEL = 1024
HEAD_DIM = 64
N_Q_HEADS = 8
N_KV_HEADS = 2
Q_GROUP = N_Q_HEADS // N_KV_HEADS
ATTN_W = N_Q_HEADS * HEAD_DIM
KV_W = N_KV_HEADS * HEAD_DIM
GROUP_W = Q_GROUP * HEAD_DIM
POOL_WINDOWS = (2, 4, 8, 16)
POOL_W = D_MODEL // 2
POOL_GROUP_W = POOL_W // len(POOL_WINDOWS)
CONV_W = D_MODEL // 2
CONV_K = 31
GRID_W = 64
ROPE_THETA = 10000.0
EPS = 1e-6
LOG2_E = 1.4426950408889634
Q_END = ATTN_W
K_END = Q_END + KV_W
V_END = K_END + KV_W
P_END = V_END + POOL_W
C_END = P_END + 2 * CONV_W
D_FF = -(-8 * D_MODEL // (3 * 256)) * 256
FF_CHUNKS = ((0, 1024), (1024, 2048), (2048, D_FF))

V7X_LANES = 128
V7X_SUBLANES = 8
V7X_MXU_DIM = 256
POOL_HALO = 8
POOL_SLAB_PAD = 128
CONV_HALO = 16
CONV_ROWS = 32
IN_PROJ_SUB_ROWS = 512
FFN_SUB_ROWS = 512
MOD_ROWS = 24
ADA_COLS = 1536
ATTN_ROWS = 512
MIXER_ROWS = 512
V7X_VMEM_BYTES = 64 * 1024 * 1024
VMEM_LIMIT = V7X_VMEM_BYTES - 8 * 1024 * 1024

F32 = jnp.float32
BF16 = jnp.bfloat16


def _const_spec(shape):
    return pl.BlockSpec(shape, lambda *_: (0,) * len(shape), pipeline_mode=pl.Buffered(1))


def _params(n_axes):
    return pltpu.CompilerParams(dimension_semantics=("arbitrary",) * n_axes,
                                vmem_limit_bytes=VMEM_LIMIT)


def _dot(a, b):
    return jnp.dot(a, b, preferred_element_type=F32)


def _gate(half_z, half_y):
    return (jnp.tanh(half_z) + 1.0) * half_y


def _modulated_rmsnorm(x, g, shift, scale):
    y = x * lax.rsqrt(jnp.mean(x * x, axis=-1, keepdims=True) + EPS)
    return (y * g) * (1.0 + scale) + shift


def _ada_kernel(c_ref, w_ref, b_ref, o_ref):
    c = c_ref[...]
    s = c * jax.nn.sigmoid(c)
    o_ref[...] = jnp.dot(s, w_ref[...], preferred_element_type=F32,
                         precision=lax.Precision.HIGHEST) + b_ref[...]


def _ada(cc, w_ada, b_ada):
    depth, d, n = w_ada.shape
    bn = ADA_COLS
    return pl.pallas_call(
        _ada_kernel,
        out_shape=jax.ShapeDtypeStruct((depth, MOD_ROWS, n), F32),
        grid=(depth, n // bn),
        in_specs=[pl.BlockSpec((MOD_ROWS, d), lambda l, j: (0, 0)),
                  pl.BlockSpec((None, d, bn), lambda l, j: (l, 0, j)),
                  pl.BlockSpec((None, 1, bn), lambda l, j: (l, 0, j))],
        out_specs=pl.BlockSpec((None, MOD_ROWS, bn), lambda l, j: (l, 0, j)),
        compiler_params=_params(2),
        name="adaln",
    )(cc, w_ada, b_ada.reshape(depth, 1, n))


def _head_rms(u, headmat, gain):
    ms = _dot((u * u).astype(BF16), headmat)
    return u * lax.rsqrt(ms + EPS) * gain


def _rope(u, cos, sin_signed):
    lane = lax.broadcasted_iota(jnp.int32, u.shape, 1)
    first_half = (lane % 32) < 16
    partner = jnp.where(first_half,
                        pltpu.roll(u, V7X_LANES - 16, 1),
                        pltpu.roll(u, 16, 1))
    return u * cos + partner * sin_signed


def _replicate_heads(a):
    lane = lax.broadcasted_iota(jnp.int32, a.shape, 1)
    low = lane < HEAD_DIM
    r = pltpu.roll(a, HEAD_DIM, 1)
    return jnp.where(low, a, r), jnp.where(low, r, a)


def _in_proj_kernel(*refs, rope, kv_only, sub_rows):
    it = iter(refs)
    x_ref, mod_ref, gpre_ref, w_ref, hm_ref, qg_ref, kg_ref = (next(it) for _ in range(7))
    cos_ref = sin_ref = None
    if rope:
        cos_ref, sin_ref = next(it), next(it)
    if kv_only:
        k_out, v_out = next(it), next(it)
    else:
        q_out, k_out, v_out, up_out, gc_out = (next(it) for _ in range(5))

    d = D_MODEL
    for r0 in range(0, x_ref.shape[0], sub_rows):
        rows = slice(r0, r0 + sub_rows)
        h = _modulated_rmsnorm(x_ref[rows, :], gpre_ref[...], mod_ref[:, 0:d], mod_ref[:, d:2 * d])
        hb = h.astype(BF16)

        if kv_only:
            ukv = _dot(hb, w_ref[...])
        else:
            ukv = _dot(hb, w_ref[:, Q_END:V_END])
        k = _head_rms(ukv[:, :KV_W], hm_ref[0:KV_W, 0:KV_W], kg_ref[...])
        if rope:
            k = _rope(k, cos_ref[rows, :], sin_ref[rows, :])
        h0, h1 = _replicate_heads(k)
        k_out[rows, 0:GROUP_W] = jnp.concatenate([h0, h0], axis=1).astype(BF16)
        k_out[rows, GROUP_W:2 * GROUP_W] = jnp.concatenate([h1, h1], axis=1).astype(BF16)
        v = ukv[:, KV_W:]
        low = lax.broadcasted_iota(jnp.int32, v.shape, 1) < HEAD_DIM
        vr = pltpu.roll(v, HEAD_DIM, 1)
        patterns = (jnp.where(low, v, 1.0), jnp.where(low, 1.0, vr),
                    jnp.where(low, vr, 1.0), jnp.where(low, 1.0, v))
        for i, pat in enumerate(patterns):
            v_out[i, rows, :] = jnp.concatenate([pat, pat], axis=1).astype(BF16)
        if kv_only:
            continue

        q = _head_rms(_dot(hb, w_ref[:, 0:Q_END]), hm_ref[...], qg_ref[...])
        for c in range(ATTN_W // V7X_LANES):
            sl = slice(c * V7X_LANES, (c + 1) * V7X_LANES)
            qc = q[:, sl]
            if rope:
                qc = _rope(qc, cos_ref[rows, :], sin_ref[rows, :])
            q_out[rows, sl] = (qc * (HEAD_DIM ** -0.5 * LOG2_E)).astype(BF16)

        up_out[rows, :] = _dot(hb, w_ref[:, V_END:P_END])
        uc = _dot(hb, w_ref[:, P_END:C_END])
        gc_out[rows, :] = _gate(uc[:, CONV_W:], uc[:, :CONV_W])


def _in_proj(x, mod, mod_row, g_pre, w, headmat, q_gain, k_gain, rope_tabs, *, tile, kv_only=False):
    bn, ln, d = x.shape
    rope = rope_tabs is not None
    if mod_row is None:
        mod_map = lambda b, j: (b, 0, 0)
    else:
        mod_map = lambda b, j: (mod_row, 0, 0)
    in_specs = [pl.BlockSpec((None, tile, d), lambda b, j: (b, j, 0)),
                pl.BlockSpec((None, 1, mod.shape[-1]), mod_map),
                _const_spec(g_pre.shape), _const_spec(w.shape), _const_spec(headmat.shape),
                _const_spec(q_gain.shape), _const_spec(k_gain.shape)]
    args = [x, mod, g_pre, w, headmat, q_gain, k_gain]
    if rope:
        in_specs += [pl.BlockSpec((tile, V7X_LANES), lambda b, j: (j, 0))] * 2
        args += list(rope_tabs)

    def tok_spec(width):
        return pl.BlockSpec((None, tile, width), lambda b, j: (b, j, 0))

    k_rep = jax.ShapeDtypeStruct((bn, ln, 2 * GROUP_W), BF16)
    v_rep = jax.ShapeDtypeStruct((bn, 2 * N_KV_HEADS, ln, GROUP_W), BF16)
    v_spec = pl.BlockSpec((None, 2 * N_KV_HEADS, tile, GROUP_W), lambda b, j: (b, 0, j, 0))
    if kv_only:
        out_shape = (k_rep, v_rep)
        out_specs = (tok_spec(2 * GROUP_W), v_spec)
    else:
        out_shape = (jax.ShapeDtypeStruct((bn, ln, ATTN_W), BF16), k_rep, v_rep,
                     jax.ShapeDtypeStruct((bn, ln, POOL_W), F32),
                     jax.ShapeDtypeStruct((bn, ln, CONV_W), F32))
        out_specs = (tok_spec(ATTN_W), tok_spec(2 * GROUP_W), v_spec,
                     tok_spec(POOL_W), tok_spec(CONV_W))
    return pl.pallas_call(
        functools.partial(_in_proj_kernel, rope=rope, kv_only=kv_only, sub_rows=min(tile, IN_PROJ_SUB_ROWS)),
        out_shape=out_shape, grid=(bn, ln // tile), in_specs=in_specs, out_specs=out_specs,
        compiler_params=_params(2),
        name="in_proj_kv" if kv_only else "in_proj",
    )(*args)


def _attn_kernel(*refs, n_src, rows):
    q_ref = refs[0]
    k_refs = refs[1:1 + n_src]
    v_refs = refs[1 + n_src:1 + 2 * n_src]
    o_ref = refs[1 + 2 * n_src]
    acc_ref, s_a, s_b, m_a, m_b = refs[2 + 2 * n_src:]
    n_chunks = q_ref.shape[0] // rows
    head_of_lane = lax.broadcasted_iota(jnp.int32, (rows, GROUP_W), 1) // HEAD_DIM
    k_widths = [k.shape[0] for k in k_refs]

    def chunk_rows(r):
        return pl.ds(r * rows, rows)

    def key_blocks():
        col = 0
        for src, width in enumerate(k_widths):
            for c0 in range(0, width, V7X_MXU_DIM):
                yield src, c0, col + c0
            col += width

    def scores(r, g, s_ref, m_ref):
        q = q_ref[chunk_rows(r), :]
        qg = jnp.where(head_of_lane == g, q, jnp.zeros_like(q))
        m = None
        for src, c0, col in key_blocks():
            s = lax.dot_general(qg, k_refs[src][c0:c0 + V7X_MXU_DIM, :], (((1,), (1,)), ((), ())),
                                preferred_element_type=F32)
            s_ref[:, col:col + V7X_MXU_DIM] = s
            blk = jnp.maximum(s[:, :V7X_LANES], s[:, V7X_LANES:])
            m = blk if m is None else jnp.maximum(m, blk)
        m_ref[...] = jnp.broadcast_to(jnp.max(m, axis=-1, keepdims=True), m_ref.shape)

    def softmax_pv(r, g, s_ref, m_ref):
        m = m_ref[...]
        lane_pattern = g % 2
        res = None
        for src, c0, col in key_blocks():
            p = [jnp.exp2(s_ref[:, col + i * V7X_LANES:col + (i + 1) * V7X_LANES] - m).astype(BF16)
                 for i in range(V7X_MXU_DIM // V7X_LANES)]
            part = _dot(jnp.concatenate(p, axis=1), v_refs[src][lane_pattern, c0:c0 + V7X_MXU_DIM, :])
            res = part if res is None else res + part
        norm = [res[:, h * V7X_LANES:(h + 1) * V7X_LANES] for h in range(GROUP_W // V7X_LANES)]
        norm = jnp.concatenate([n / pltpu.roll(n, HEAD_DIM, 1) for n in norm], axis=1)
        mine = head_of_lane == g
        if g == 0:
            acc_ref[...] = jnp.where(mine, norm, 0.0)
        elif g < Q_GROUP - 1:
            acc_ref[...] = jnp.where(mine, norm, acc_ref[...])
        else:
            o_ref[chunk_rows(r), :] = jnp.where(mine, norm, acc_ref[...]).astype(o_ref.dtype)

    bufs = ((s_a, m_a), (s_b, m_b))
    scores(0, 0, *bufs[0])
    for r in range(n_chunks):
        for g in range(Q_GROUP):
            if g + 1 < Q_GROUP:
                scores(r, g + 1, *bufs[(g + 1) % 2])
            elif r + 1 < n_chunks:
                scores(r + 1, 0, *bufs[0])
            softmax_pv(r, g, *bufs[g % 2])


def _attention(q, k_srcs, v_srcs, *, rows):
    bn, ln, _ = q.shape
    n_src = len(k_srcs)
    n_keys = sum(k.shape[1] for k in k_srcs)
    q_spec = pl.BlockSpec((None, ln, GROUP_W), lambda b, h: (b, 0, h))
    kv_specs = ([pl.BlockSpec((None, k.shape[1], GROUP_W), lambda b, h: (b, 0, h)) for k in k_srcs]
                + [pl.BlockSpec((None, 2, v.shape[2], GROUP_W), lambda b, h: (b, h, 0, 0)) for v in v_srcs])
    return pl.pallas_call(
        functools.partial(_attn_kernel, n_src=n_src, rows=rows),
        out_shape=jax.ShapeDtypeStruct((bn, ln, ATTN_W), BF16),
        grid=(bn, N_KV_HEADS),
        in_specs=[q_spec] + kv_specs, out_specs=q_spec,
        scratch_shapes=[pltpu.VMEM((rows, GROUP_W), F32),
                        pltpu.VMEM((rows, n_keys), F32), pltpu.VMEM((rows, n_keys), F32),
                        pltpu.VMEM((rows, V7X_LANES), F32), pltpu.VMEM((rows, V7X_LANES), F32)],
        compiler_params=_params(2),
        name="attention",
    )(q, *k_srcs, *v_srcs)


def _mixer_kernel(x_ref, o_ref, up_ref, up_prev_ref, up_next_ref, gc_ref, gc_prev_ref, gc_next_ref,
                  icnt_ref, mod_ref, band_ref, gpre_ref, gpost_ref, wgate_ref, wao_ref, poolw_ref, pscale_ref,
                  wpo_ref, cdw_ref, cb_ref, lng_ref, lnb_ref, wco_ref, wout_ref,
                  out_ref, ppad_ref, gpad_ref, gsh_ref, cacc_ref):
    tile = x_ref.shape[0]
    d = D_MODEL
    j = pl.program_id(1)
    first = j == 0
    last = j == pl.num_programs(1) - 1

    gpad_ref[0:CONV_HALO, :] = jnp.where(first, 0.0, gc_prev_ref[...])
    gpad_ref[CONV_HALO:CONV_HALO + tile, :] = gc_ref[...]
    gpad_ref[CONV_HALO + tile:, :] = jnp.where(last, 0.0, gc_next_ref[...])
    tap0 = CONV_HALO - CONV_K // 2
    shifted_rows = gsh_ref.shape[1]
    for r in range(V7X_SUBLANES):
        gsh_ref[r] = gpad_ref[r:r + shifted_rows, :]
    groups = CONV_ROWS // V7X_SUBLANES
    for r0 in range(0, tile, CONV_ROWS):
        acc = jnp.broadcast_to(cb_ref[...], (groups, V7X_SUBLANES, CONV_W))
        for k in range(CONV_K):
            phase = (tap0 + k) % V7X_SUBLANES
            base = (tap0 + k) - phase
            g = gsh_ref[phase, r0 + base:r0 + base + CONV_ROWS, :].reshape(groups, V7X_SUBLANES, CONV_W)
            acc = acc + cdw_ref[k * V7X_SUBLANES:(k + 1) * V7X_SUBLANES, :][None] * g
        cacc_ref[r0:r0 + CONV_ROWS, :] = acc.reshape(CONV_ROWS, CONV_W)
    yc = cacc_ref[...]
    cen = yc - jnp.mean(yc, axis=-1, keepdims=True)
    var = jnp.mean(cen * cen, axis=-1, keepdims=True)
    yc = cen * lax.rsqrt(var + EPS) * lng_ref[...] + lnb_ref[...]
    yc = _gate(yc, yc).astype(BF16)

    x = x_ref[...]
    hb = _modulated_rmsnorm(x, gpre_ref[...], mod_ref[:, 0:d], mod_ref[:, d:2 * d]).astype(BF16)

    merged = _gate(_dot(hb, wgate_ref[:, 0:d]), _dot(o_ref[...], wao_ref[...]))

    ppad_ref[0:POOL_HALO, :] = jnp.where(first, 0.0, up_prev_ref[...])
    ppad_ref[POOL_HALO:POOL_HALO + tile, :] = up_ref[...]
    ppad_ref[POOL_HALO + tile:2 * POOL_HALO + tile, :] = jnp.where(last, 0.0, up_next_ref[...])
    ppad_ref[2 * POOL_HALO + tile:, :] = jnp.zeros((ppad_ref.shape[0] - 2 * POOL_HALO - tile, POOL_W), F32)
    pb = ppad_ref[...].astype(BF16)
    pooled = []
    for gi in range(len(POOL_WINDOWS)):
        cols = slice(gi * POOL_GROUP_W, (gi + 1) * POOL_GROUP_W)
        dg = _dot(band_ref[gi], pb[:, cols]) * icnt_ref[:, cols] - up_ref[:, cols]
        pooled.append(_dot(dg.astype(BF16), poolw_ref[gi]))
    yb = jnp.concatenate(pooled, axis=-1) * pscale_ref[...]
    merged = merged + _gate(_dot(hb, wgate_ref[:, d:2 * d]), _dot(yb.astype(BF16), wpo_ref[...]))
    merged = merged + _gate(_dot(hb, wgate_ref[:, 2 * d:3 * d]), _dot(yc, wco_ref[...]))

    y = _dot(merged.astype(BF16), wout_ref[...])
    yn = y * lax.rsqrt(jnp.mean(y * y, axis=-1, keepdims=True) + EPS) * gpost_ref[...]
    out_ref[...] = x + mod_ref[:, 2 * d:3 * d] * yn


def _pool_tables(n_tokens, tile):
    slab = tile + POOL_SLAB_PAD
    t = jnp.arange(tile)[:, None]
    i = jnp.arange(slab)[None, :] - POOL_HALO
    band = jnp.stack([(i >= t - w // 2) & (i < t + w // 2) for w in POOL_WINDOWS]).astype(BF16)
    pos = jnp.arange(n_tokens)
    inv = [1.0 / (jnp.minimum(pos + w // 2, n_tokens) - jnp.maximum(pos - w // 2, 0)).astype(F32)
           for w in POOL_WINDOWS]
    return band, jnp.repeat(jnp.stack(inv, axis=-1), POOL_GROUP_W, axis=-1)


def _mixer(x, attn, up, gc, mod, mod_row, lw, *, tile):
    bn, ln, d = x.shape
    n_tiles = ln // tile
    if mod_row is None:
        mod_map = lambda b, j: (b, 0, 0)
    else:
        mod_map = lambda b, j: (mod_row, 0, 0)

    def tok_spec(width):
        return pl.BlockSpec((None, tile, width), lambda b, j: (b, j, 0))

    def halo_specs(width, halo):
        per_tile = tile // halo
        n_blocks = ln // halo
        prev = pl.BlockSpec((None, halo, width), lambda b, j: (b, jnp.maximum(j * per_tile - 1, 0), 0))
        nxt = pl.BlockSpec((None, halo, width),
                           lambda b, j: (b, jnp.minimum((j + 1) * per_tile, n_blocks - 1), 0))
        return [tok_spec(width), prev, nxt]

    band, inv_cnt = _pool_tables(ln, tile)
    weights = [band, lw["g_pre1"], lw["g_post1"], lw["w_gate"], lw["w_attn_o"], lw["pool_w"], lw["pool_scale"],
               lw["w_pool_o"], lw["conv_dw"], lw["conv_b"], lw["conv_ln_g"], lw["conv_ln_b"],
               lw["w_conv_o"], lw["w_out"]]
    in_specs = ([tok_spec(d), tok_spec(ATTN_W)] + halo_specs(POOL_W, POOL_HALO) + halo_specs(CONV_W, CONV_HALO)
                + [pl.BlockSpec((tile, POOL_W), lambda b, j: (j, 0)),
                   pl.BlockSpec((None, 1, mod.shape[-1]), mod_map)] + [_const_spec(w.shape) for w in weights])
    return pl.pallas_call(
        _mixer_kernel,
        out_shape=jax.ShapeDtypeStruct(x.shape, F32),
        grid=(bn, n_tiles), in_specs=in_specs, out_specs=tok_spec(d),
        scratch_shapes=[pltpu.VMEM((band.shape[-1], POOL_W), F32),
                        pltpu.VMEM((tile + 2 * CONV_HALO, CONV_W), F32),
                        pltpu.VMEM((V7X_SUBLANES, tile + 2 * CONV_HALO - V7X_SUBLANES, CONV_W), F32),
                        pltpu.VMEM((tile, CONV_W), F32)],
        compiler_params=_params(2),
        name="mixer",
    )(x, attn, up, up, up, gc, gc, gc, inv_cnt, mod, *weights)


def _ffn_kernel(x_ref, mod_ref, gpre_ref, gpost_ref, w1_ref, w2_ref, out_ref, *, sub_rows):
    d = D_MODEL
    for r0 in range(0, x_ref.shape[0], sub_rows):
        rows = slice(r0, r0 + sub_rows)
        x = x_ref[rows, :]
        hb = _modulated_rmsnorm(x, gpre_ref[...], mod_ref[:, 3 * d:4 * d], mod_ref[:, 4 * d:5 * d]).astype(BF16)
        y = None
        for lo, hi in FF_CHUNKS:
            a = _dot(hb, w1_ref[:, lo:hi])
            b = _dot(hb, w1_ref[:, D_FF + lo:D_FF + hi])
            part = _dot((_gate(a, a) * b).astype(BF16), w2_ref[lo:hi, :])
            y = part if y is None else y + part
        yn = y * lax.rsqrt(jnp.mean(y * y, axis=-1, keepdims=True) + EPS) * gpost_ref[...]
        out_ref[rows, :] = x + mod_ref[:, 5 * d:6 * d] * yn


def _ffn(x, mod, mod_row, g_pre, g_post, w1, w2, *, tile):
    bn, ln, d = x.shape
    if mod_row is None:
        mod_map = lambda b, j: (b, 0, 0)
    else:
        mod_map = lambda b, j: (mod_row, 0, 0)
    tok = pl.BlockSpec((None, tile, d), lambda b, j: (b, j, 0))
    return pl.pallas_call(
        functools.partial(_ffn_kernel, sub_rows=min(tile, FFN_SUB_ROWS)),
        out_shape=jax.ShapeDtypeStruct(x.shape, F32),
        grid=(bn, ln // tile),
        in_specs=[tok, pl.BlockSpec((None, 1, mod.shape[-1]), mod_map),
                  _const_spec(g_pre.shape), _const_spec(g_post.shape),
                  _const_spec(w1.shape), _const_spec(w2.shape)],
        out_specs=tok,
        compiler_params=_params(2),
        name="ffn",
    )(x, mod, g_pre, g_post, w1, w2)


def _rope_tables(n_tokens):
    t = jnp.arange(n_tokens)
    row = (t // GRID_W).astype(F32)
    col = (t % GRID_W).astype(F32)
    n_freq = HEAD_DIM // 4
    inv = 1.0 / (ROPE_THETA ** (jnp.arange(n_freq, dtype=F32) / n_freq))
    ang_r = row[:, None] * inv[None, :]
    ang_c = col[:, None] * inv[None, :]
    cos = jnp.concatenate([jnp.cos(ang_r)] * 2 + [jnp.cos(ang_c)] * 2, axis=-1)
    sin = jnp.concatenate([-jnp.sin(ang_r), jnp.sin(ang_r), -jnp.sin(ang_c), jnp.sin(ang_c)], axis=-1)
    return jnp.tile(cos, (1, 2)), jnp.tile(sin, (1, 2))


def kernel(x, c, ctx, c_ctx, w_ada, b_ada, g_pre1, g_post1, g_pre2, g_post2, w_in, q_norm, k_norm,
           w_attn_o, pool_w, pool_scale, w_pool_o, conv_dw, conv_b, conv_ln_g, conv_ln_b, w_conv_o,
           w_out, w_ffn_in, w_ffn_out):
    batch, n_lat, d = x.shape
    n_ctx = ctx.shape[1]
    depth = w_ada.shape[0]
    ctx_row = batch

    cc = jnp.zeros((MOD_ROWS, d), F32).at[:batch].set(c).at[ctx_row].set(c_ctx)
    mod_all = _ada(cc, w_ada, b_ada).reshape(depth, MOD_ROWS, 1, 6 * d)

    rope = _rope_tables(n_lat)
    lane_head = jnp.arange(ATTN_W) // HEAD_DIM
    headmat = jnp.where(lane_head[:, None] == lane_head[None, :], 1.0 / HEAD_DIM, 0.0).astype(BF16)

    ctx_tile = n_ctx
    e = ctx
    for i in range(depth):
        last = i == depth - 1
        mod = mod_all[i]
        row = lambda a: a[i].reshape(1, -1)
        half_from = lambda n, start: jnp.where(jnp.arange(n) >= start, 0.5, 1.0).astype(F32)
        w_in_b = (w_in[i] * half_from(w_in.shape[-1], P_END)).astype(BF16)
        w_proj = w_in_b[:, :C_END]
        q_gain = jnp.tile(q_norm[i], N_Q_HEADS).reshape(1, -1)
        k_gain = jnp.tile(k_norm[i], N_KV_HEADS).reshape(1, -1)
        lw = {"g_pre1": row(g_pre1), "g_post1": row(g_post1), "w_gate": w_in_b[:, C_END:],
              "w_attn_o": (0.5 * w_attn_o[i]).astype(BF16), "pool_w": pool_w[i].astype(BF16),
              "pool_scale": row(pool_scale), "w_pool_o": (0.5 * w_pool_o[i]).astype(BF16),
              "conv_dw": jnp.repeat(conv_dw[i], V7X_SUBLANES, axis=0), "conv_b": row(conv_b),
              "conv_ln_g": 0.5 * row(conv_ln_g), "conv_ln_b": 0.5 * row(conv_ln_b),
              "w_conv_o": (0.5 * w_conv_o[i]).astype(BF16), "w_out": w_out[i].astype(BF16)}
        w1 = (w_ffn_in[i] * (1.5 - half_from(w_ffn_in.shape[-1], D_FF))).astype(BF16)
        w2 = w_ffn_out[i].astype(BF16)

        if last:
            kc, vc = _in_proj(e, mod, ctx_row, lw["g_pre1"], w_proj[:, Q_END:V_END], headmat, q_gain, k_gain,
                              None, tile=ctx_tile, kv_only=True)
        else:
            qc, kc, vc, upc, gcc = _in_proj(e, mod, ctx_row, lw["g_pre1"], w_proj, headmat, q_gain, k_gain,
                                            None, tile=ctx_tile)
        qx, kx, vx, upx, gcx = _in_proj(x, mod, None, lw["g_pre1"], w_proj, headmat, q_gain, k_gain,
                                        rope, tile=2 * IN_PROJ_SUB_ROWS)
        ax = _attention(qx, (kc, kx), (vc, vx), rows=ATTN_ROWS)
        x = _mixer(x, ax, upx, gcx, mod, None, lw, tile=MIXER_ROWS)
        if not last:
            ac = _attention(qc, (kc,), (vc,), rows=ctx_tile)
            e = _mixer(e, ac, upc, gcc, mod, ctx_row, lw, tile=ctx_tile)

        x = _ffn(x, mod, None, row(g_pre2), row(g_post2), w1, w2, tile=2 * FFN_SUB_ROWS)
        if not last:
            e = _ffn(e, mod, ctx_row, row(g_pre2), row(g_post2), w1, w2, tile=ctx_tile)
    return x
```

```python
import functools

import jax
import jax.numpy as jnp
from jax import lax
from jax.experimental import pallas as pl
from jax.experimental.pallas import tpu as pltpu

D_MODEL = 1024
HEAD_DIM = 64
N_Q_HEADS = 8
N_KV_HEADS = 2
Q_GROUP = N_Q_HEADS // N_KV_HEADS
ATTN_W = N_Q_HEADS * HEAD_DIM
KV_W = N_KV_HEADS * HEAD_DIM
GROUP_W = Q_GROUP * HEAD_DIM
POOL_WINDOWS = (2, 4, 8, 16)
POOL_W = D_MODEL // 2
POOL_GROUP_W = POOL_W // len(POOL_WINDOWS)
CONV_W = D_MODEL // 2
CONV_K = 31
GRID_W = 64
ROPE_THETA = 10000.0
EPS = 1e-6
LOG2_E = 1.4426950408889634
Q_END = ATTN_W
K_END = Q_END + KV_W
V_END = K_END + KV_W
P_END = V_END + POOL_W
C_END = P_END + 2 * CONV_W
D_FF = -(-8 * D_MODEL // (3 * 256)) * 256
FF_CHUNKS = ((0, 1024), (1024, 2048), (2048, D_FF))

V7X_LANES = 128
V7X_SUBLANES = 8
V7X_MXU_DIM = 256
POOL_HALO = 8
POOL_SLAB_PAD = 128
CONV_HALO = 16
CONV_ROWS = 32
IN_PROJ_SUB_ROWS = 512
FFN_SUB_ROWS = 512
MOD_ROWS = 24
ADA_COLS = 1536
ATTN_ROWS = 512
MIXER_ROWS = 512
V7X_VMEM_BYTES = 64 * 1024 * 1024
VMEM_LIMIT = V7X_VMEM_BYTES - 8 * 1024 * 1024

F32 = jnp.float32
BF16 = jnp.bfloat16


def _const_spec(shape):
    return pl.BlockSpec(shape, lambda *_: (0,) * len(shape), pipeline_mode=pl.Buffered(1))


def _params(n_axes):
    return pltpu.CompilerParams(dimension_semantics=("arbitrary",) * n_axes,
                                vmem_limit_bytes=VMEM_LIMIT)


def _dot(a, b):
    return jnp.dot(a, b, preferred_element_type=F32)


def _gate(half_z, half_y):
    return (jnp.tanh(half_z) + 1.0) * half_y


def _modulated_rmsnorm(x, g, shift, scale):
    y = x * lax.rsqrt(jnp.mean(x * x, axis=-1, keepdims=True) + EPS)
    return (y * g) * (1.0 + scale) + shift


def _ada_kernel(c_ref, w_ref, b_ref, o_ref):
    c = c_ref[...]
    s = c * jax.nn.sigmoid(c)
    o_ref[...] = jnp.dot(s, w_ref[...], preferred_element_type=F32,
                         precision=lax.Precision.HIGHEST) + b_ref[...]


def _ada(cc, w_ada, b_ada):
    depth, d, n = w_ada.shape
    bn = ADA_COLS
    return pl.pallas_call(
        _ada_kernel,
        out_shape=jax.ShapeDtypeStruct((depth, MOD_ROWS, n), F32),
        grid=(depth, n // bn),
        in_specs=[pl.BlockSpec((MOD_ROWS, d), lambda l, j: (0, 0)),
                  pl.BlockSpec((None, d, bn), lambda l, j: (l, 0, j)),
                  pl.BlockSpec((None, 1, bn), lambda l, j: (l, 0, j))],
        out_specs=pl.BlockSpec((None, MOD_ROWS, bn), lambda l, j: (l, 0, j)),
        compiler_params=_params(2),
        name="adaln",
    )(cc, w_ada, b_ada.reshape(depth, 1, n))


def _head_rms(u, headmat, gain):
    ms = _dot((u * u).astype(BF16), headmat)
    return u * lax.rsqrt(ms + EPS) * gain


def _rope(u, cos, sin_signed):
    lane = lax.broadcasted_iota(jnp.int32, u.shape, 1)
    first_half = (lane % 32) < 16
    partner = jnp.where(first_half,
                        pltpu.roll(u, V7X_LANES - 16, 1),
                        pltpu.roll(u, 16, 1))
    return u * cos + partner * sin_signed


def _replicate_heads(a):
    lane = lax.broadcasted_iota(jnp.int32, a.shape, 1)
    low = lane < HEAD_DIM
    r = pltpu.roll(a, HEAD_DIM, 1)
    return jnp.where(low, a, r), jnp.where(low, r, a)


def _in_proj_kernel(*refs, rope, kv_only, sub_rows):
    it = iter(refs)
    x_ref, mod_ref, gpre_ref, w_ref, hm_ref, qg_ref, kg_ref = (next(it) for _ in range(7))
    cos_ref = sin_ref = None
    if rope:
        cos_ref, sin_ref = next(it), next(it)
    if kv_only:
        k_out, v_out = next(it), next(it)
    else:
        q_out, k_out, v_out, up_out, gc_out = (next(it) for _ in range(5))

    d = D_MODEL
    for r0 in range(0, x_ref.shape[0], sub_rows):
        rows = slice(r0, r0 + sub_rows)
        h = _modulated_rmsnorm(x_ref[rows, :], gpre_ref[...], mod_ref[:, 0:d], mod_ref[:, d:2 * d])
        hb = h.astype(BF16)

        if kv_only:
            ukv = _dot(hb, w_ref[...])
        else:
            ukv = _dot(hb, w_ref[:, Q_END:V_END])
        k = _head_rms(ukv[:, :KV_W], hm_ref[0:KV_W, 0:KV_W], kg_ref[...])
        if rope:
            k = _rope(k, cos_ref[rows, :], sin_ref[rows, :])
        h0, h1 = _replicate_heads(k)
        k_out[rows, 0:GROUP_W] = jnp.concatenate([h0, h0], axis=1).astype(BF16)
        k_out[rows, GROUP_W:2 * GROUP_W] = jnp.concatenate([h1, h1], axis=1).astype(BF16)
        v = ukv[:, KV_W:]
        low = lax.broadcasted_iota(jnp.int32, v.shape, 1) < HEAD_DIM
        vr = pltpu.roll(v, HEAD_DIM, 1)
        patterns = (jnp.where(low, v, 1.0), jnp.where(low, 1.0, vr),
                    jnp.where(low, vr, 1.0), jnp.where(low, 1.0, v))
        for i, pat in enumerate(patterns):
            v_out[i, rows, :] = jnp.concatenate([pat, pat], axis=1).astype(BF16)
        if kv_only:
            continue

        q = _head_rms(_dot(hb, w_ref[:, 0:Q_END]), hm_ref[...], qg_ref[...])
        for c in range(ATTN_W // V7X_LANES):
            sl = slice(c * V7X_LANES, (c + 1) * V7X_LANES)
            qc = q[:, sl]
            if rope:
                qc = _rope(qc, cos_ref[rows, :], sin_ref[rows, :])
            q_out[rows, sl] = (qc * (HEAD_DIM ** -0.5 * LOG2_E)).astype(BF16)

        up_out[rows, :] = _dot(hb, w_ref[:, V_END:P_END])
        uc = _dot(hb, w_ref[:, P_END:C_END])
        gc_out[rows, :] = _gate(uc[:, CONV_W:], uc[:, :CONV_W])


def _in_proj(x, mod, mod_row, g_pre, w, headmat, q_gain, k_gain, rope_tabs, *, tile, kv_only=False):
    bn, ln, d = x.shape
    rope = rope_tabs is not None
    if mod_row is None:
        mod_map = lambda b, j: (b, 0, 0)
    else:
        mod_map = lambda b, j: (mod_row, 0, 0)
    if kv_only:
        w_spec = pl.BlockSpec((d, V_END - Q_END), lambda b, j: (0, Q_END // (V_END - Q_END)),
                              pipeline_mode=pl.Buffered(1))
    else:
        w_spec = pl.BlockSpec((d, C_END), lambda b, j: (0, 0), pipeline_mode=pl.Buffered(1))
    in_specs = [pl.BlockSpec((None, tile, d), lambda b, j: (b, j, 0)),
                pl.BlockSpec((None, 1, mod.shape[-1]), mod_map),
                _const_spec(g_pre.shape), w_spec, _const_spec(headmat.shape),
                _const_spec(q_gain.shape), _const_spec(k_gain.shape)]
    args = [x, mod, g_pre, w, headmat, q_gain, k_gain]
    if rope:
        in_specs += [pl.BlockSpec((tile, V7X_LANES), lambda b, j: (j, 0))] * 2
        args += list(rope_tabs)

    def tok_spec(width):
        return pl.BlockSpec((None, tile, width), lambda b, j: (b, j, 0))

    k_rep = jax.ShapeDtypeStruct((bn, ln, 2 * GROUP_W), BF16)
    v_rep = jax.ShapeDtypeStruct((bn, 2 * N_KV_HEADS, ln, GROUP_W), BF16)
    v_spec = pl.BlockSpec((None, 2 * N_KV_HEADS, tile, GROUP_W), lambda b, j: (b, 0, j, 0))
    if kv_only:
        out_shape = (k_rep, v_rep)
        out_specs = (tok_spec(2 * GROUP_W), v_spec)
    else:
        out_shape = (jax.ShapeDtypeStruct((bn, ln, ATTN_W), BF16), k_rep, v_rep,
                     jax.ShapeDtypeStruct((bn, ln, POOL_W), F32),
                     jax.ShapeDtypeStruct((bn, ln, CONV_W), F32))
        out_specs = (tok_spec(ATTN_W), tok_spec(2 * GROUP_W), v_spec,
                     tok_spec(POOL_W), tok_spec(CONV_W))
    return pl.pallas_call(
        functools.partial(_in_proj_kernel, rope=rope, kv_only=kv_only, sub_rows=min(tile, IN_PROJ_SUB_ROWS)),
        out_shape=out_shape, grid=(bn, ln // tile), in_specs=in_specs, out_specs=out_specs,
        compiler_params=_params(2),
        name="in_proj_kv" if kv_only else "in_proj",
    )(*args)


def _attn_kernel(*refs, n_src, rows):
    q_ref = refs[0]
    k_refs = refs[1:1 + n_src]
    v_refs = refs[1 + n_src:1 + 2 * n_src]
    o_ref = refs[1 + 2 * n_src]
    acc_ref, s_a, s_b, m_a, m_b = refs[2 + 2 * n_src:]
    n_chunks = q_ref.shape[0] // rows
    head_of_lane = lax.broadcasted_iota(jnp.int32, (rows, GROUP_W), 1) // HEAD_DIM
    k_widths = [k.shape[0] for k in k_refs]

    def chunk_rows(r):
        return pl.ds(r * rows, rows)

    def key_blocks():
        col = 0
        for src, width in enumerate(k_widths):
            for c0 in range(0, width, V7X_MXU_DIM):
                yield src, c0, col + c0
            col += width

    def scores(r, g, s_ref, m_ref):
        q = q_ref[chunk_rows(r), :]
        qg = jnp.where(head_of_lane == g, q, jnp.zeros_like(q))
        m = None
        for src, c0, col in key_blocks():
            s = lax.dot_general(qg, k_refs[src][c0:c0 + V7X_MXU_DIM, :], (((1,), (1,)), ((), ())),
                                preferred_element_type=F32)
            s_ref[:, col:col + V7X_MXU_DIM] = s
            blk = jnp.maximum(s[:, :V7X_LANES], s[:, V7X_LANES:])
            m = blk if m is None else jnp.maximum(m, blk)
        m_ref[...] = jnp.broadcast_to(jnp.max(m, axis=-1, keepdims=True), m_ref.shape)

    def softmax_pv(r, g, s_ref, m_ref):
        m = m_ref[...]
        lane_pattern = g % 2
        res = None
        for src, c0, col in key_blocks():
            p = [jnp.exp2(s_ref[:, col + i * V7X_LANES:col + (i + 1) * V7X_LANES] - m).astype(BF16)
                 for i in range(V7X_MXU_DIM // V7X_LANES)]
            part = _dot(jnp.concatenate(p, axis=1), v_refs[src][lane_pattern, c0:c0 + V7X_MXU_DIM, :])
            res = part if res is None else res + part
        norm = [res[:, h * V7X_LANES:(h + 1) * V7X_LANES] for h in range(GROUP_W // V7X_LANES)]
        norm = jnp.concatenate([n / pltpu.roll(n, HEAD_DIM, 1) for n in norm], axis=1)
        mine = head_of_lane == g
        if g == 0:
            acc_ref[...] = jnp.where(mine, norm, 0.0)
        elif g < Q_GROUP - 1:
            acc_ref[...] = jnp.where(mine, norm, acc_ref[...])
        else:
            o_ref[chunk_rows(r), :] = jnp.where(mine, norm, acc_ref[...]).astype(o_ref.dtype)

    bufs = ((s_a, m_a), (s_b, m_b))
    scores(0, 0, *bufs[0])
    for r in range(n_chunks):
        for g in range(Q_GROUP):
            if g + 1 < Q_GROUP:
                scores(r, g + 1, *bufs[(g + 1) % 2])
            elif r + 1 < n_chunks:
                scores(r + 1, 0, *bufs[0])
            softmax_pv(r, g, *bufs[g % 2])


def _attention(q, k_srcs, v_srcs, *, rows):
    bn, ln, _ = q.shape
    n_src = len(k_srcs)
    n_keys = sum(k.shape[1] for k in k_srcs)
    q_spec = pl.BlockSpec((None, ln, GROUP_W), lambda b, h: (b, 0, h))
    kv_specs = ([pl.BlockSpec((None, k.shape[1], GROUP_W), lambda b, h: (b, 0, h)) for k in k_srcs]
                + [pl.BlockSpec((None, 2, v.shape[2], GROUP_W), lambda b, h: (b, h, 0, 0)) for v in v_srcs])
    return pl.pallas_call(
        functools.partial(_attn_kernel, n_src=n_src, rows=rows),
        out_shape=jax.ShapeDtypeStruct((bn, ln, ATTN_W), BF16),
        grid=(bn, N_KV_HEADS),
        in_specs=[q_spec] + kv_specs, out_specs=q_spec,
        scratch_shapes=[pltpu.VMEM((rows, GROUP_W), F32),
                        pltpu.VMEM((rows, n_keys), F32), pltpu.VMEM((rows, n_keys), F32),
                        pltpu.VMEM((rows, V7X_LANES), F32), pltpu.VMEM((rows, V7X_LANES), F32)],
        compiler_params=_params(2),
        name="attention",
    )(q, *k_srcs, *v_srcs)


def _mixer_kernel(x_ref, o_ref, up_ref, up_prev_ref, up_next_ref, gc_ref, gc_prev_ref, gc_next_ref,
                  icnt_ref, mod_ref, band_ref, gpre_ref, gpost_ref, wgate_ref, wao_ref, poolw_ref, pscale_ref,
                  wpo_ref, cdw_ref, cb_ref, lng_ref, lnb_ref, wco_ref, wout_ref,
                  out_ref, ppad_ref, gpad_ref, gsh_ref, cacc_ref):
    tile = x_ref.shape[0]
    d = D_MODEL
    j = pl.program_id(1)
    first = j == 0
    last = j == pl.num_programs(1) - 1

    gpad_ref[0:CONV_HALO, :] = jnp.where(first, 0.0, gc_prev_ref[...])
    gpad_ref[CONV_HALO:CONV_HALO + tile, :] = gc_ref[...]
    gpad_ref[CONV_HALO + tile:, :] = jnp.where(last, 0.0, gc_next_ref[...])
    tap0 = CONV_HALO - CONV_K // 2
    shifted_rows = gsh_ref.shape[1]
    for r in range(V7X_SUBLANES):
        gsh_ref[r] = gpad_ref[r:r + shifted_rows, :]
    groups = CONV_ROWS // V7X_SUBLANES
    for r0 in range(0, tile, CONV_ROWS):
        acc = jnp.broadcast_to(cb_ref[...], (groups, V7X_SUBLANES, CONV_W))
        for k in range(CONV_K):
            phase = (tap0 + k) % V7X_SUBLANES
            base = (tap0 + k) - phase
            g = gsh_ref[phase, r0 + base:r0 + base + CONV_ROWS, :].reshape(groups, V7X_SUBLANES, CONV_W)
            acc = acc + cdw_ref[k * V7X_SUBLANES:(k + 1) * V7X_SUBLANES, :][None] * g
        cacc_ref[r0:r0 + CONV_ROWS, :] = acc.reshape(CONV_ROWS, CONV_W)
    yc = cacc_ref[...]
    cen = yc - jnp.mean(yc, axis=-1, keepdims=True)
    var = jnp.mean(cen * cen, axis=-1, keepdims=True)
    yc = cen * lax.rsqrt(var + EPS) * lng_ref[...] + lnb_ref[...]
    yc = _gate(yc, yc).astype(BF16)

    x = x_ref[...]
    hb = _modulated_rmsnorm(x, gpre_ref[...], mod_ref[:, 0:d], mod_ref[:, d:2 * d]).astype(BF16)

    merged = _gate(_dot(hb, wgate_ref[:, 0:d]), _dot(o_ref[...], wao_ref[...]))

    ppad_ref[0:POOL_HALO, :] = jnp.where(first, 0.0, up_prev_ref[...])
    ppad_ref[POOL_HALO:POOL_HALO + tile, :] = up_ref[...]
    ppad_ref[POOL_HALO + tile:2 * POOL_HALO + tile, :] = jnp.where(last, 0.0, up_next_ref[...])
    ppad_ref[2 * POOL_HALO + tile:, :] = jnp.zeros((ppad_ref.shape[0] - 2 * POOL_HALO - tile, POOL_W), F32)
    pb = ppad_ref[...].astype(BF16)
    pooled = []
    for gi in range(len(POOL_WINDOWS)):
        cols = slice(gi * POOL_GROUP_W, (gi + 1) * POOL_GROUP_W)
        dg = _dot(band_ref[gi], pb[:, cols]) * icnt_ref[:, cols] - up_ref[:, cols]
        pooled.append(_dot(dg.astype(BF16), poolw_ref[gi]))
    yb = jnp.concatenate(pooled, axis=-1) * pscale_ref[...]
    merged = merged + _gate(_dot(hb, wgate_ref[:, d:2 * d]), _dot(yb.astype(BF16), wpo_ref[...]))
    merged = merged + _gate(_dot(hb, wgate_ref[:, 2 * d:3 * d]), _dot(yc, wco_ref[...]))

    y = _dot(merged.astype(BF16), wout_ref[...])
    yn = y * lax.rsqrt(jnp.mean(y * y, axis=-1, keepdims=True) + EPS) * gpost_ref[...]
    out_ref[...] = x + mod_ref[:, 2 * d:3 * d] * yn


def _pool_tables(n_tokens, tile):
    slab = tile + POOL_SLAB_PAD
    t = jnp.arange(tile)[:, None]
    i = jnp.arange(slab)[None, :] - POOL_HALO
    band = jnp.stack([(i >= t - w // 2) & (i < t + w // 2) for w in POOL_WINDOWS]).astype(BF16)
    pos = jnp.arange(n_tokens)
    inv = [1.0 / (jnp.minimum(pos + w // 2, n_tokens) - jnp.maximum(pos - w // 2, 0)).astype(F32)
           for w in POOL_WINDOWS]
    return band, jnp.repeat(jnp.stack(inv, axis=-1), POOL_GROUP_W, axis=-1)


def _mixer(x, attn, up, gc, mod, mod_row, lw, *, tile):
    bn, ln, d = x.shape
    n_tiles = ln // tile
    if mod_row is None:
        mod_map = lambda b, j: (b, 0, 0)
    else:
        mod_map = lambda b, j: (mod_row, 0, 0)

    def tok_spec(width):
        return pl.BlockSpec((None, tile, width), lambda b, j: (b, j, 0))

    def halo_specs(width, halo):
        per_tile = tile // halo
        n_blocks = ln // halo
        prev = pl.BlockSpec((None, halo, width), lambda b, j: (b, jnp.maximum(j * per_tile - 1, 0), 0))
        nxt = pl.BlockSpec((None, halo, width),
                           lambda b, j: (b, jnp.minimum((j + 1) * per_tile, n_blocks - 1), 0))
        return [tok_spec(width), prev, nxt]

    band, inv_cnt = _pool_tables(ln, tile)
    weights = [band, lw["g_pre1"], lw["g_post1"], lw["w_gate"], lw["w_attn_o"], lw["pool_w"], lw["pool_scale"],
               lw["w_pool_o"], lw["conv_dw"], lw["conv_b"], lw["conv_ln_g"], lw["conv_ln_b"],
               lw["w_conv_o"], lw["w_out"]]
    in_specs = ([tok_spec(d), tok_spec(ATTN_W)] + halo_specs(POOL_W, POOL_HALO) + halo_specs(CONV_W, CONV_HALO)
                + [pl.BlockSpec((tile, POOL_W), lambda b, j: (j, 0)),
                   pl.BlockSpec((None, 1, mod.shape[-1]), mod_map)] + [_const_spec(w.shape) for w in weights])
    return pl.pallas_call(
        _mixer_kernel,
        out_shape=jax.ShapeDtypeStruct(x.shape, F32),
        grid=(bn, n_tiles), in_specs=in_specs, out_specs=tok_spec(d),
        scratch_shapes=[pltpu.VMEM((band.shape[-1], POOL_W), F32),
                        pltpu.VMEM((tile + 2 * CONV_HALO, CONV_W), F32),
                        pltpu.VMEM((V7X_SUBLANES, tile + 2 * CONV_HALO - V7X_SUBLANES, CONV_W), F32),
                        pltpu.VMEM((tile, CONV_W), F32)],
        compiler_params=_params(2),
        name="mixer",
    )(x, attn, up, up, up, gc, gc, gc, inv_cnt, mod, *weights)


def _ffn_kernel(x_ref, mod_ref, gpre_ref, gpost_ref, w1_ref, w2_ref, out_ref, *, sub_rows):
    d = D_MODEL
    for r0 in range(0, x_ref.shape[0], sub_rows):
        rows = slice(r0, r0 + sub_rows)
        x = x_ref[rows, :]
        hb = _modulated_rmsnorm(x, gpre_ref[...], mod_ref[:, 3 * d:4 * d], mod_ref[:, 4 * d:5 * d]).astype(BF16)
        y = None
        for lo, hi in FF_CHUNKS:
            a = _dot(hb, w1_ref[:, lo:hi])
            b = _dot(hb, w1_ref[:, D_FF + lo:D_FF + hi])
            part = _dot((_gate(a, a) * b).astype(BF16), w2_ref[lo:hi, :])
            y = part if y is None else y + part
        yn = y * lax.rsqrt(jnp.mean(y * y, axis=-1, keepdims=True) + EPS) * gpost_ref[...]
        out_ref[rows, :] = x + mod_ref[:, 5 * d:6 * d] * yn


def _ffn(x, mod, mod_row, g_pre, g_post, w1, w2, *, tile):
    bn, ln, d = x.shape
    if mod_row is None:
        mod_map = lambda b, j: (b, 0, 0)
    else:
        mod_map = lambda b, j: (mod_row, 0, 0)
    tok = pl.BlockSpec((None, tile, d), lambda b, j: (b, j, 0))
    return pl.pallas_call(
        functools.partial(_ffn_kernel, sub_rows=min(tile, FFN_SUB_ROWS)),
        out_shape=jax.ShapeDtypeStruct(x.shape, F32),
        grid=(bn, ln // tile),
        in_specs=[tok, pl.BlockSpec((None, 1, mod.shape[-1]), mod_map),
                  _const_spec(g_pre.shape), _const_spec(g_post.shape),
                  _const_spec(w1.shape), _const_spec(w2.shape)],
        out_specs=tok,
        compiler_params=_params(2),
        name="ffn",
    )(x, mod, g_pre, g_post, w1, w2)


def _rope_tables(n_tokens):
    t = jnp.arange(n_tokens)
    row = (t // GRID_W).astype(F32)
    col = (t % GRID_W).astype(F32)
    n_freq = HEAD_DIM // 4
    inv = 1.0 / (ROPE_THETA ** (jnp.arange(n_freq, dtype=F32) / n_freq))
    ang_r = row[:, None] * inv[None, :]
    ang_c = col[:, None] * inv[None, :]
    cos = jnp.concatenate([jnp.cos(ang_r)] * 2 + [jnp.cos(ang_c)] * 2, axis=-1)
    sin = jnp.concatenate([-jnp.sin(ang_r), jnp.sin(ang_r), -jnp.sin(ang_c), jnp.sin(ang_c)], axis=-1)
    return jnp.tile(cos, (1, 2)), jnp.tile(sin, (1, 2))


def kernel(x, c, ctx, c_ctx, w_ada, b_ada, g_pre1, g_post1, g_pre2, g_post2, w_in, q_norm, k_norm,
           w_attn_o, pool_w, pool_scale, w_pool_o, conv_dw, conv_b, conv_ln_g, conv_ln_b, w_conv_o,
           w_out, w_ffn_in, w_ffn_out):
    batch, n_lat, d = x.shape
    n_ctx = ctx.shape[1]
    depth = w_ada.shape[0]
    ctx_row = batch

    cc = jnp.zeros((MOD_ROWS, d), F32).at[:batch].set(c).at[ctx_row].set(c_ctx)
    mod_all = _ada(cc, w_ada, b_ada).reshape(depth, MOD_ROWS, 1, 6 * d)

    rope = _rope_tables(n_lat)
    lane_head = jnp.arange(ATTN_W) // HEAD_DIM
    headmat = jnp.where(lane_head[:, None] == lane_head[None, :], 1.0 / HEAD_DIM, 0.0).astype(BF16)

    ctx_tile = n_ctx
    e = ctx
    for i in range(depth):
        last = i == depth - 1
        mod = mod_all[i]
        row = lambda a: a[i].reshape(1, -1)
        half_from = lambda n, start: jnp.where(jnp.arange(n) >= start, 0.5, 1.0).astype(F32)
        w_in_b = (w_in[i] * half_from(w_in.shape[-1], P_END)).astype(BF16)
        q_gain = jnp.tile(q_norm[i], N_Q_HEADS).reshape(1, -1)
        k_gain = jnp.tile(k_norm[i], N_KV_HEADS).reshape(1, -1)
        lw = {"g_pre1": row(g_pre1), "g_post1": row(g_post1), "w_gate": w_in_b[:, C_END:],
              "w_attn_o": (0.5 * w_attn_o[i]).astype(BF16), "pool_w": pool_w[i].astype(BF16),
              "pool_scale": row(pool_scale), "w_pool_o": (0.5 * w_pool_o[i]).astype(BF16),
              "conv_dw": jnp.repeat(conv_dw[i], V7X_SUBLANES, axis=0), "conv_b": row(conv_b),
              "conv_ln_g": 0.5 * row(conv_ln_g), "conv_ln_b": 0.5 * row(conv_ln_b),
              "w_conv_o": (0.5 * w_conv_o[i]).astype(BF16), "w_out": w_out[i].astype(BF16)}
        w1 = (w_ffn_in[i] * (1.5 - half_from(w_ffn_in.shape[-1], D_FF))).astype(BF16)
        w2 = w_ffn_out[i].astype(BF16)

        if last:
            kc, vc = _in_proj(e, mod, ctx_row, lw["g_pre1"], w_in_b, headmat, q_gain, k_gain,
                              None, tile=ctx_tile, kv_only=True)
        else:
            qc, kc, vc, upc, gcc = _in_proj(e, mod, ctx_row, lw["g_pre1"], w_in_b, headmat, q_gain, k_gain,
                                            None, tile=ctx_tile)
        qx, kx, vx, upx, gcx = _in_proj(x, mod, None, lw["g_pre1"], w_in_b, headmat, q_gain, k_gain,
                                        rope, tile=2 * IN_PROJ_SUB_ROWS)
        ax = _attention(qx, (kc, kx), (vc, vx), rows=ATTN_ROWS)
        x = _mixer(x, ax, upx, gcx, mod, None, lw, tile=MIXER_ROWS)
        if not last:
            ac = _attention(qc, (kc,), (vc,), rows=ctx_tile)
            e = _mixer(e, ac, upc, gcc, mod, ctx_row, lw, tile=ctx_tile)

        x = _ffn(x, mod, None, row(g_pre2), row(g_post2), w1, w2, tile=2 * FFN_SUB_ROWS)
        if not last:
            e = _ffn(e, mod, ctx_row, row(g_pre2), row(g_post2), w1, w2, tile=ctx_tile)
    return x
```
